```python
import math
import jax, jax.numpy as jnp
from jax import lax
import numpy as np

D_MODEL = 1024
BATCH = 8
SEQ = 4096
DEPTH = 2

EPS = 1e-6
NEG_BIG = -1e30
CONV_A_WIDTH = D_MODEL
CONV_A_K = 3
ATTN_HEADS = 16
HEAD_DIM = 64
ATTN_WIDTH = ATTN_HEADS * HEAD_DIM
ROT_DIM = HEAD_DIM // 4
ROPE_THETA = 500000.0
DILATED_PATTERNS = ((128, 1), (512, 4), (2048, 16))
Q_BLOCK = 128
SSD_EXPAND = 2
SSD_WIDTH = SSD_EXPAND * D_MODEL
SSD_HEAD_DIM = 64
SSD_HEADS = SSD_WIDTH // SSD_HEAD_DIM
SSD_GROUPS = 4
SSD_STATE = 128
SSD_CONV_K = 5
SSD_CHUNK = 128
SSD_CONV_DIM = SSD_WIDTH + 2 * SSD_GROUPS * SSD_STATE
N_BRANCH = 3
D_FF = 4 * D_MODEL
IN_SIZES = (3 * CONV_A_WIDTH, 3 * ATTN_WIDTH, SSD_WIDTH, SSD_CONV_DIM, 2 * SSD_HEADS, N_BRANCH * D_MODEL)
D_IN_PROJ = 3 * CONV_A_WIDTH + 3 * ATTN_WIDTH + SSD_WIDTH + SSD_CONV_DIM + 2 * SSD_HEADS + N_BRANCH * D_MODEL

kernel_name = "hybrid_conv_dilattn_ssd_encoder"


def _split_points(sizes):
    return tuple(int(v) for v in np.cumsum(sizes)[:-1])


def rmsnorm(x, w):
    x32 = x.astype(jnp.float32)
    y = x32 * lax.rsqrt(jnp.mean(x32 * x32, axis=-1, keepdims=True) + EPS)
    return (y * w.astype(jnp.float32)).astype(x.dtype)


def depthwise_conv_centred(x, w):
    k, c = w.shape
    return lax.conv_general_dilated(
        x, w.astype(x.dtype).reshape(k, 1, c), window_strides=(1,), padding=[(k // 2, k // 2)],
        dimension_numbers=("NWC", "WIO", "NWC"), feature_group_count=c)


def short_conv_mixer(a_in, conv_w):
    b_gate, c_gate, xh = jnp.split(a_in, 3, axis=-1)
    return b_gate * depthwise_conv_centred(c_gate * xh, conv_w)


def partial_rotary(t, positions):
    half = ROT_DIM // 2
    inv_freq = jnp.power(ROPE_THETA, -jnp.arange(0, ROT_DIM, 2, dtype=jnp.float32) / ROT_DIM)
    ang = positions.astype(jnp.float32)[..., None] * inv_freq
    cos = jnp.cos(ang)[:, :, None, :]
    sin = jnp.sin(ang)[:, :, None, :]
    t1 = t[..., :half].astype(jnp.float32)
    t2 = t[..., half:ROT_DIM].astype(jnp.float32)
    rot = jnp.concatenate([t1 * cos - t2 * sin, t2 * cos + t1 * sin], axis=-1).astype(t.dtype)
    return jnp.concatenate([rot, t[..., ROT_DIM:]], axis=-1)


def banded_attention(q, k, v, half_window):
    n, length, dh = q.shape
    n_blk = -(-length // Q_BLOCK)
    l_pad = n_blk * Q_BLOCK
    span = Q_BLOCK + 2 * half_window
    qb = jnp.pad(q, ((0, 0), (0, l_pad - length), (0, 0))).reshape(n, n_blk, Q_BLOCK, dh)
    pad_k = ((0, 0), (half_window, half_window + l_pad - length), (0, 0))
    kp = jnp.pad(k, pad_k)
    vp = jnp.pad(v, pad_k)
    key_idx = jnp.arange(n_blk)[:, None] * Q_BLOCK + jnp.arange(span)[None, :]
    kb = kp[:, key_idx]
    vb = vp[:, key_idx]
    s = jnp.einsum("nbqd,nbkd->nbqk", qb, kb, preferred_element_type=jnp.float32) * (dh ** -0.5)
    rel = jnp.arange(span)[None, :] - jnp.arange(Q_BLOCK)[:, None]
    in_band = (rel >= 0) & (rel <= 2 * half_window)
    key_pos = key_idx - half_window
    key_ok = (key_pos >= 0) & (key_pos < length)
    mask = in_band[None, :, :] & key_ok[:, None, :]
    s = jnp.where(mask[None], s, NEG_BIG)
    m = jnp.max(s, axis=-1, keepdims=True)
    p = jnp.exp(s - m)
    den = jnp.sum(p, axis=-1, keepdims=True)
    o = jnp.einsum("nbqk,nbkd->nbqd", p, vb.astype(jnp.float32)) / den
    lse = (m + jnp.log(den))[..., 0]
    return o.reshape(n, l_pad, dh)[:, :length], lse.reshape(n, l_pad)[:, :length]


def dilated_attention(q, k, v):
    b, s, h, dh = q.shape
    outs, lses = [], []
    for window, dil in DILATED_PATTERNS:
        half = window // (2 * dil)
        length = s // dil

        def to_sub(t):
            t = t.reshape(b, length, dil, h, dh).transpose(0, 2, 3, 1, 4)
            return t.reshape(b * dil * h, length, dh)

        o, lse = banded_attention(to_sub(q), to_sub(k), to_sub(v), half)
        outs.append(o.reshape(b, dil, h, length, dh).transpose(0, 3, 1, 2, 4).reshape(b, s, h, dh))
        lses.append(lse.reshape(b, dil, h, length).transpose(0, 3, 1, 2).reshape(b, s, h))
    wts = jax.nn.softmax(jnp.stack(lses, axis=-1), axis=-1)
    o = wts[..., 0, None] * outs[0]
    for i in range(1, len(outs)):
        o = o + wts[..., i, None] * outs[i]
    return o.astype(q.dtype)


def ssd_scan(x, dt, a, bm, cm):
    bt, s, h, p = x.shape
    g = SSD_GROUPS
    r = h // g
    n = bm.shape[-1]
    t = SSD_CHUNK
    nc = s // t
    xd = (x * dt[..., None]).reshape(bt, nc, t, g, r, p)
    la = (dt * a).reshape(bt, nc, t, g, r)
    bc = bm.reshape(bt, nc, t, g, n)
    cc = cm.reshape(bt, nc, t, g, n)
    cum = jnp.cumsum(la, axis=2)
    tri = jnp.tril(jnp.ones((t, t), dtype=bool))[None, None, :, :, None, None]
    seg = cum[:, :, :, None] - cum[:, :, None, :]
    decay = jnp.exp(jnp.where(tri, seg, -jnp.inf))
    cb = jnp.einsum("bclgn,bcsgn->bclsg", cc, bc)
    y_diag = jnp.einsum("bclsgr,bcsgrp->bclgrp", cb[..., None] * decay, xd)
    xw = xd * jnp.exp(cum[:, :, -1:] - cum)[..., None]
    states = jnp.einsum("bclgn,bclgrp->bcgrpn", bc, xw)
    chunk_decay = jnp.exp(cum[:, :, -1])

    def step(hs, inp):
        st, dec = inp
        return hs * dec[..., None, None] + st, hs

    h0 = jnp.zeros((bt, g, r, p, n), dtype=x.dtype)
    _, h_in = lax.scan(step, h0, (jnp.moveaxis(states, 1, 0), jnp.moveaxis(chunk_decay, 1, 0)))
    h_in = jnp.moveaxis(h_in, 0, 1)
    y_off = jnp.einsum("bclgn,bcgrpn->bclgrp", cc, h_in) * jnp.exp(cum)[..., None]
    return (y_diag + y_off).reshape(bt, s, h, p)


def ssd_mixer(z, xbc, dt_raw, conv_w, conv_b, a_log, dt_bias, d_skip, norm_w):
    bt, s, _ = z.shape
    xbc = jax.nn.silu(depthwise_conv_centred(xbc, conv_w) + conv_b.astype(xbc.dtype))
    xs, bm, cm = jnp.split(xbc, _split_points((SSD_WIDTH, SSD_GROUPS * SSD_STATE, SSD_GROUPS * SSD_STATE)), axis=-1)
    xh = xs.astype(jnp.float32).reshape(bt, s, SSD_HEADS, SSD_HEAD_DIM)
    bm = bm.astype(jnp.float32).reshape(bt, s, SSD_GROUPS, SSD_STATE)
    cm = cm.astype(jnp.float32).reshape(bt, s, SSD_GROUPS, SSD_STATE)
    dt = jax.nn.softplus(dt_raw.astype(jnp.float32).reshape(bt, s, 2, SSD_HEADS) + dt_bias.astype(jnp.float32))
    a = -jnp.exp(a_log.astype(jnp.float32))
    y_fwd = ssd_scan(xh, dt[:, :, 0], a[0], bm, cm)
    flip = lambda u: jnp.flip(u, axis=1)
    y_bwd = flip(ssd_scan(flip(xh), flip(dt[:, :, 1]), a[1], flip(bm), flip(cm)))
    y = y_fwd + y_bwd + xh * d_skip.astype(jnp.float32)[:, None]
    y = y.reshape(bt, s, SSD_WIDTH) * jax.nn.silu(z.astype(jnp.float32))
    yg = y.reshape(bt, s, SSD_GROUPS, SSD_WIDTH // SSD_GROUPS)
    yg = yg * lax.rsqrt(jnp.mean(yg * yg, axis=-1, keepdims=True) + EPS)
    y = yg.reshape(bt, s, SSD_WIDTH) * norm_w.astype(jnp.float32)
    return y.astype(z.dtype)


def hybrid_layer(x, positions, mix_norm_w, w_in, conv_a_w, w_a_out, w_b_out, ssd_conv_w, ssd_conv_b,
                 ssd_a_log, ssd_dt_bias, ssd_d, ssd_norm_w, w_c_out, w_o, mlp_norm_w, w_ff1, w_ff2):
    b, s, _ = x.shape
    u = rmsnorm(x, mix_norm_w)
    proj = u @ w_in
    a_in, qkv, z, xbc, dt_raw, gate_logits = jnp.split(proj, _split_points(IN_SIZES), axis=-1)
    y_a = short_conv_mixer(a_in, conv_a_w) @ w_a_out
    q, k, v = jnp.split(qkv, 3, axis=-1)
    q = partial_rotary(q.reshape(b, s, ATTN_HEADS, HEAD_DIM), positions)
    k = partial_rotary(k.reshape(b, s, ATTN_HEADS, HEAD_DIM), positions)
    v = v.reshape(b, s, ATTN_HEADS, HEAD_DIM)
    y_b = dilated_attention(q, k, v).reshape(b, s, ATTN_WIDTH) @ w_b_out
    y_c = ssd_mixer(z, xbc, dt_raw, ssd_conv_w, ssd_conv_b, ssd_a_log, ssd_dt_bias, ssd_d, ssd_norm_w) @ w_c_out
    g = jax.nn.sigmoid(gate_logits.astype(jnp.float32)).astype(x.dtype)
    g_a, g_b, g_c = jnp.split(g, N_BRANCH, axis=-1)
    x = x + (g_a * y_a + g_b * y_b + g_c * y_c) @ w_o
    h = rmsnorm(x, mlp_norm_w)
    return x + jnp.square(jax.nn.relu(h @ w_ff1)) @ w_ff2


def setup_inputs(seed: int = 0) -> dict:
    key = jax.random.key(seed)
    ks = jax.random.split(key, 20)
    f32 = jnp.float32
    nrm = lambda k, shape, scale: jax.random.normal(k, shape, f32) * scale
    x = jax.random.normal(ks[0], (BATCH, SEQ, D_MODEL), f32)
    offsets = jax.random.randint(ks[1], (BATCH, 1), 0, 1024, dtype=jnp.int32)
    positions = offsets + jnp.arange(SEQ, dtype=jnp.int32)[None, :]
    dt_init = jnp.exp(jax.random.uniform(ks[9], (DEPTH, 2, SSD_HEADS), f32, math.log(1e-3), math.log(1e-1)))
    return {
        "x": x,
        "positions": positions,
        "mix_norm_w": 1.0 + nrm(ks[2], (DEPTH, D_MODEL), 0.02),
        "w_in": nrm(ks[3], (DEPTH, D_MODEL, D_IN_PROJ), D_MODEL ** -0.5),
        "conv_a_w": nrm(ks[4], (DEPTH, CONV_A_K, CONV_A_WIDTH), CONV_A_K ** -0.5),
        "w_a_out": nrm(ks[5], (DEPTH, CONV_A_WIDTH, D_MODEL), CONV_A_WIDTH ** -0.5),
        "w_b_out": nrm(ks[6], (DEPTH, ATTN_WIDTH, D_MODEL), ATTN_WIDTH ** -0.5),
        "ssd_conv_w": nrm(ks[7], (DEPTH, SSD_CONV_K, SSD_CONV_DIM), SSD_CONV_K ** -0.5),
        "ssd_conv_b": nrm(ks[8], (DEPTH, SSD_CONV_DIM), 0.02),
        "ssd_a_log": jnp.log(jax.random.uniform(ks[10], (DEPTH, 2, SSD_HEADS), f32, 1.0, 16.0)),
        "ssd_dt_bias": dt_init + jnp.log(-jnp.expm1(-dt_init)),
        "ssd_d": 1.0 + nrm(ks[11], (DEPTH, SSD_HEADS), 0.02),
        "ssd_norm_w": 1.0 + nrm(ks[12], (DEPTH, SSD_WIDTH), 0.02),
        "w_c_out": nrm(ks[13], (DEPTH, SSD_WIDTH, D_MODEL), SSD_WIDTH ** -0.5),
        "w_o": nrm(ks[14], (DEPTH, D_MODEL, D_MODEL), D_MODEL ** -0.5),
        "mlp_norm_w": 1.0 + nrm(ks[15], (DEPTH, D_MODEL), 0.02),
        "w_ff1": nrm(ks[16], (DEPTH, D_MODEL, D_FF), D_MODEL ** -0.5),
        "w_ff2": nrm(ks[17], (DEPTH, D_FF, D_MODEL), D_FF ** -0.5),
        "final_norm_w": 1.0 + nrm(ks[18], (D_MODEL,), 0.02),
    }


def reference(x, positions, mix_norm_w, w_in, conv_a_w, w_a_out, w_b_out, ssd_conv_w, ssd_conv_b,
              ssd_a_log, ssd_dt_bias, ssd_d, ssd_norm_w, w_c_out, w_o, mlp_norm_w, w_ff1, w_ff2, final_norm_w):
    for l in range(DEPTH):
        x = hybrid_layer(x, positions, mix_norm_w[l], w_in[l], conv_a_w[l], w_a_out[l], w_b_out[l],
                         ssd_conv_w[l], ssd_conv_b[l], ssd_a_log[l], ssd_dt_bias[l], ssd_d[l], ssd_norm_w[l],
                         w_c_out[l], w_o[l], mlp_norm_w[l], w_ff1[l], w_ff2[l])
    return rmsnorm(x, final_norm_w)
```

```python
import functools

import jax
import jax.numpy as jnp
from jax import lax
from jax.experimental import pallas as pl
from jax.experimental.pallas import tpu as pltpu

F32 = jnp.float32
BF16 = jnp.bfloat16

D_MODEL = 1024
EPS = 1e-6
NEG_BIG = -1e30
N_HEADS = 16
HEAD_DIM = 64
ROT_DIM = 16
ROPE_THETA = 500000.0
DILATIONS = (1, 4, 16)
HALF_WINDOW = 64
SSD_WIDTH = 2048
SSD_HEADS = 32
SSD_HEAD_DIM = 64
SSD_GROUPS = 4
SSD_STATE = 128
SSD_CHUNK = 128
SSD_CONV_DIM = SSD_WIDTH + 2 * SSD_GROUPS * SSD_STATE
D_FF = 4 * D_MODEL

N_PROJ = 14336
COL_A = 0
COL_QKV = 3072
COL_Z = 6144
COL_XBC = 8192
COL_GATE = 11264
PROJ_TN = 512

VMEM_LIMIT = 56 * 1024 * 1024
HALO = 16


def _cparams(sem):
    return pltpu.CompilerParams(dimension_semantics=sem, vmem_limit_bytes=VMEM_LIMIT)


def _dot(a, b):
    return jnp.dot(a, b, preferred_element_type=F32)


def _dot_nt(a, b):
    return lax.dot_general(a, b, (((1,), (1,)), ((), ())), preferred_element_type=F32)


def _split2(v):
    hi = v.astype(BF16)
    lo = (v - hi.astype(F32)).astype(BF16)
    return hi, lo


def _split3(v):
    hi = v.astype(BF16)
    r = v - hi.astype(F32)
    mid = r.astype(BF16)
    lo = (r - mid.astype(F32)).astype(BF16)
    return hi, mid, lo


def _inproj_kernel(x_ref, nw_ref, w_ref, wdh_ref, wdl_ref, cos_ref, sa_ref, sb_ref,
                   proj_ref, dt_ref, u_ref):
    j = pl.program_id(1)

    @pl.when(j == 0)
    def _():
        x = x_ref[...]
        u = x * lax.rsqrt(jnp.mean(x * x, axis=-1, keepdims=True) + EPS) * nw_ref[...]
        uh, ul = _split2(u)
        u_ref[...] = uh
        dt_ref[...] = _dot(uh, wdh_ref[...]) + _dot(uh, wdl_ref[...]) + _dot(ul, wdh_ref[...])

    acc = _dot(u_ref[...], w_ref[...])
    q_lo, k_lo, k_hi = COL_QKV // PROJ_TN, (COL_QKV + 1024) // PROJ_TN, (COL_QKV + 2048) // PROJ_TN
    g_lo = COL_GATE // PROJ_TN
    is_rot = (j >= q_lo) & (j < k_hi)
    is_gate = j >= g_lo

    @pl.when(is_rot)
    def _():
        reps = PROJ_TN // 128
        c = jnp.tile(cos_ref[...], (1, reps))
        sa = jnp.tile(sa_ref[...], (1, reps))
        sb = jnp.tile(sb_ref[...], (1, reps))
        r = acc * c + pltpu.roll(acc, PROJ_TN - ROT_DIM // 2, 1) * sa + pltpu.roll(acc, ROT_DIM // 2, 1) * sb
        scale = jnp.where(j < k_lo, HEAD_DIM ** -0.5, 1.0).astype(F32)
        proj_ref[...] = (r * scale).astype(BF16)

    @pl.when(is_gate)
    def _():
        proj_ref[...] = jax.nn.sigmoid(acc).astype(BF16)

    @pl.when(jnp.logical_not(is_rot | is_gate))
    def _():
        proj_ref[...] = acc.astype(BF16)


def _inproj(x2, nw, w_main, wd_hi, wd_lo, cos_t, sa_t, sb_t, tm):
    t = x2.shape[0]
    grid = (t // tm, N_PROJ // PROJ_TN)
    row = lambda i, j: (i, 0)
    return pl.pallas_call(
        _inproj_kernel,
        grid=grid,
        in_specs=[
            pl.BlockSpec((tm, D_MODEL), row),
            pl.BlockSpec((1, D_MODEL), lambda i, j: (0, 0)),
            pl.BlockSpec((D_MODEL, PROJ_TN), lambda i, j: (0, j)),
            pl.BlockSpec((D_MODEL, 128), lambda i, j: (0, 0)),
            pl.BlockSpec((D_MODEL, 128), lambda i, j: (0, 0)),
            pl.BlockSpec((tm, 128), row),
            pl.BlockSpec((tm, 128), row),
            pl.BlockSpec((tm, 128), row),
        ],
        out_specs=[
            pl.BlockSpec((tm, PROJ_TN), lambda i, j: (i, j)),
            pl.BlockSpec((tm, 128), row),
        ],
        out_shape=[
            jax.ShapeDtypeStruct((t, N_PROJ), BF16),
            jax.ShapeDtypeStruct((t, 128), F32),
        ],
        scratch_shapes=[pltpu.VMEM((tm, D_MODEL), BF16)],
        compiler_params=_cparams(("parallel", "arbitrary")),
        name="inproj",
    )(x2, nw, w_main, wd_hi, wd_lo, cos_t, sa_t, sb_t)


def _shift_rows(x, prev, nxt, off):
    ts = x.shape[0]
    if off == 0:
        return x
    row = lax.broadcasted_iota(jnp.int32, x.shape, 0)
    y = pltpu.roll(x, (-off) % ts, 0)
    if off < 0:
        k = -off
        for r in range(k):
            src = prev.shape[0] - k + r
            y = jnp.where(row == r, prev[src:src + 1], y)
    else:
        for r in range(off):
            y = jnp.where(row == ts - off + r, nxt[r:r + 1], y)
    return y


def _ssdconv_kernel(xc_ref, xp_ref, xn_ref, w_ref, b_ref, o_ref):
    i = pl.program_id(1)
    n = pl.num_programs(1)
    x = xc_ref[0].astype(F32)
    prev = jnp.where(i > 0, xp_ref[0].astype(F32), 0.0)
    nxt = jnp.where(i < n - 1, xn_ref[0].astype(F32), 0.0)
    w = w_ref[...]
    k = w.shape[0]
    y = b_ref[...]
    for tap in range(k):
        y = y + _shift_rows(x, prev, nxt, tap - k // 2) * w[tap:tap + 1]
    o_ref[0] = (y * jax.nn.sigmoid(y)).astype(BF16)


def _ssdconv(proj3, conv_w, conv_b, ts):
    b, s, _ = proj3.shape
    nblk = SSD_CONV_DIM // 1024
    col0 = COL_XBC // 1024
    hb = ts // HALO
    nhalo = s // HALO
    return pl.pallas_call(
        _ssdconv_kernel,
        grid=(b, s // ts, nblk),
        in_specs=[
            pl.BlockSpec((1, ts, 1024), lambda bi, i, c: (bi, i, col0 + c)),
            pl.BlockSpec((1, HALO, 1024), lambda bi, i, c: (bi, jnp.maximum(i * hb - 1, 0), col0 + c)),
            pl.BlockSpec((1, HALO, 1024), lambda bi, i, c: (bi, jnp.minimum((i + 1) * hb, nhalo - 1), col0 + c)),
            pl.BlockSpec((conv_w.shape[0], 1024), lambda bi, i, c: (0, c)),
            pl.BlockSpec((1, 1024), lambda bi, i, c: (0, c)),
        ],
        out_specs=pl.BlockSpec((1, ts, 1024), lambda bi, i, c: (bi, i, c)),
        out_shape=jax.ShapeDtypeStruct((b, s, SSD_CONV_DIM), BF16),
        compiler_params=_cparams(("parallel", "parallel", "parallel")),
        name="ssdconv",
    )(proj3, proj3, proj3, conv_w, conv_b)


def _softplus(v):
    return jnp.maximum(v, 0.0) + jnp.log(1.0 + jnp.exp(-jnp.abs(v)))


def _ssd_direction(xbc_ref, dtraw, bias, a, expand, lane_off, reverse, state_ref, out_ref):
    t = SSD_CHUNK
    li = lax.broadcasted_iota(jnp.int32, (t, t), 0)
    si = lax.broadcasted_iota(jnp.int32, (t, t), 1)
    keep = (si >= li) if reverse else (si <= li)
    tri = jnp.where(keep, 1.0, 0.0).astype(BF16)

    dt = _softplus(dtraw + bias)
    la = dt * a
    h3 = _split3(la)
    cum = _dot(tri, h3[0]) + _dot(tri, h3[1]) + _dot(tri, h3[2])
    cum_t = cum.T
    edge = cum[0:1, :] if reverse else cum[t - 1:t, :]
    ecum = jnp.exp(cum)
    wst = jnp.exp(edge - cum)

    def expand2(v):
        hi, lo = _split2(v)
        return _dot(hi, expand) + _dot(lo, expand)

    dtx = expand2(dt)
    ecx = expand2(ecum)
    wsx = expand2(wst)
    xd = xbc_ref[0, :, 0:SSD_WIDTH].astype(F32) * dtx
    xdb = xd.astype(BF16)
    xw = (xd * wsx).astype(BF16)
    edge_row = t - 1 if not reverse else 0
    chunk_decay = ecx[edge_row:edge_row + 1, :]

    lane = lax.broadcasted_iota(jnp.int32, (t, 128), 1)
    gw = SSD_WIDTH // SSD_GROUPS
    for g in range(SSD_GROUPS):
        bg = xbc_ref[0, :, SSD_WIDTH + g * SSD_STATE:SSD_WIDTH + (g + 1) * SSD_STATE]
        cg = xbc_ref[0, :, SSD_WIDTH + (SSD_GROUPS + g) * SSD_STATE:SSD_WIDTH + (SSD_GROUPS + g + 1) * SSD_STATE]
        cb = _dot_nt(cg, bg)
        st = state_ref[g]
        yoff = _dot(cg, st.astype(BF16)) * ecx[:, g * gw:(g + 1) * gw]
        for pair in range(gw // 128):
            h0 = g * (gw // SSD_HEAD_DIM) + 2 * pair
            c0 = h0 * SSD_HEAD_DIM
            xdp = xdb[:, c0:c0 + 128]
            acc = yoff[:, pair * 128:(pair + 1) * 128]
            for sub in range(2):
                hl = lane_off + h0 + sub
                seg = cum[:, hl:hl + 1] - cum_t[hl:hl + 1, :]
                decay = jnp.exp(jnp.where(keep, seg, -jnp.inf))
                m = (cb * decay).astype(BF16)
                half = (lane < 64) if sub == 0 else (lane >= 64)
                acc = acc + _dot(m, jnp.where(half, xdp, jnp.zeros_like(xdp)))
            out_ref[0, :, c0:c0 + 128] = acc.astype(BF16)
        bg_t = bg.astype(F32).T.astype(BF16)
        state_ref[g] = st * chunk_decay[:, g * gw:(g + 1) * gw] + _dot(bg_t, xw[:, g * gw:(g + 1) * gw])


def _ssdscan_kernel(xf_ref, xb_ref, dtf_ref, dtb_ref, bias_ref, a_ref, ef_ref, eb_ref,
                    yf_ref, yb_ref, sf_ref, sb_ref):
    @pl.when(pl.program_id(1) == 0)
    def _():
        sf_ref[...] = jnp.zeros_like(sf_ref)
        sb_ref[...] = jnp.zeros_like(sb_ref)

    bias = bias_ref[...]
    a = a_ref[...]
    _ssd_direction(xf_ref, dtf_ref[0], bias, a, ef_ref[...], 0, False, sf_ref, yf_ref)
    _ssd_direction(xb_ref, dtb_ref[0], bias, a, eb_ref[...], SSD_HEADS, True, sb_ref, yb_ref)


def _ssdscan(xbc_c, dt_raw3, bias128, a128, exp_f, exp_b):
    b, s, _ = xbc_c.shape
    nc = s // SSD_CHUNK
    fwd = lambda bi, c: (bi, c, 0)
    bwd = lambda bi, c: (bi, nc - 1 - c, 0)
    const = lambda bi, c: (0, 0)
    state = pltpu.VMEM((SSD_GROUPS, SSD_STATE, SSD_WIDTH // SSD_GROUPS), F32)
    return pl.pallas_call(
        _ssdscan_kernel,
        grid=(b, nc),
        in_specs=[
            pl.BlockSpec((1, SSD_CHUNK, SSD_CONV_DIM), fwd),
            pl.BlockSpec((1, SSD_CHUNK, SSD_CONV_DIM), bwd),
            pl.BlockSpec((1, SSD_CHUNK, 128), fwd),
            pl.BlockSpec((1, SSD_CHUNK, 128), bwd),
            pl.BlockSpec((1, 128), const),
            pl.BlockSpec((1, 128), const),
            pl.BlockSpec((128, SSD_WIDTH), const),
            pl.BlockSpec((128, SSD_WIDTH), const),
        ],
        out_specs=[
            pl.BlockSpec((1, SSD_CHUNK, SSD_WIDTH), fwd),
            pl.BlockSpec((1, SSD_CHUNK, SSD_WIDTH), bwd),
        ],
        out_shape=[
            jax.ShapeDtypeStruct((b, s, SSD_WIDTH), BF16),
            jax.ShapeDtypeStruct((b, s, SSD_WIDTH), BF16),
        ],
        scratch_shapes=[state, state],
        compiler_params=_cparams(("parallel", "arbitrary")),
        name="ssdscan",
    )(xbc_c, xbc_c, dt_raw3, dt_raw3, bias128, a128, exp_f, exp_b)


def _attn_kernel(*refs, tq, has_prev, has_lse):
    q_ref, kp_ref, kc_ref, kn_ref, vp_ref, vc_ref, vn_ref = refs[:7]
    pos = 7
    if has_prev:
        op_ref, lp_ref = refs[pos:pos + 2]
        pos += 2
    o_ref = refs[pos]
    pos += 1
    if has_lse:
        l_ref = refs[pos]
        pos += 1
    kbuf, vbuf = refs[pos:pos + 2]

    i = pl.program_id(2)
    n = pl.num_programs(2)
    hw = HALF_WINDOW
    span = tq + 2 * hw
    kbuf[0:hw] = kp_ref[0]
    kbuf[hw:hw + tq] = kc_ref[0]
    kbuf[hw + tq:span] = kn_ref[0]
    vbuf[0:hw] = vp_ref[0]
    vbuf[hw:hw + tq] = vc_ref[0]
    vbuf[hw + tq:span] = vn_ref[0]

    ql = lax.broadcasted_iota(jnp.int32, (tq, span), 0)
    kj = lax.broadcasted_iota(jnp.int32, (tq, span), 1)
    lo = jnp.where(i == 0, hw, 0)
    hi = jnp.where(i == n - 1, hw + tq, span)
    ok = (kj >= ql) & (kj <= ql + 2 * hw) & (kj >= lo) & (kj < hi)
    bias = jnp.where(ok, 0.0, NEG_BIG).astype(F32)

    lane = lax.broadcasted_iota(jnp.int32, (tq, 128), 1)
    if has_prev:
        lprev = lp_ref[0]
    lse_tile = jnp.zeros((tq, 128), F32)
    for pair in range(N_HEADS // 2):
        outs = []
        for sub in range(2):
            h = 2 * pair + sub
            c0 = h * HEAD_DIM
            s = _dot_nt(q_ref[0, :, c0:c0 + HEAD_DIM], kbuf[:, c0:c0 + HEAD_DIM]) + bias
            m = jnp.max(s, axis=-1, keepdims=True)
            p = jnp.exp(s - m)
            den = jnp.sum(p, axis=-1, keepdims=True)
            acc = _dot(p.astype(BF16), vbuf[:, c0:c0 + HEAD_DIM])
            if has_prev:
                lse_p = lprev[:, h:h + 1]
                mm = jnp.maximum(lse_p, m)
                a1 = jnp.exp(lse_p - mm)
                a2 = jnp.exp(m - mm)
                den = a1 + den * a2
                acc = op_ref[0, :, c0:c0 + HEAD_DIM].astype(F32) * a1 + acc * a2
                m = mm
            outs.append(acc / den)
            if has_lse:
                lse_tile = jnp.where(lane == h, m + jnp.log(den), lse_tile)
        o_ref[0, :, pair * 128:(pair + 1) * 128] = jnp.concatenate(outs, axis=1).astype(BF16)
    if has_lse:
        l_ref[0] = lse_tile


def _attn_pattern(proj3, dil, o_prev, lse_prev, has_lse, tq):
    b, s, _ = proj3.shape
    length = s // dil
    ncol = N_PROJ // 1024
    qc, kc, vc = COL_QKV // 1024, COL_QKV // 1024 + 1, COL_QKV // 1024 + 2
    pview = proj3.reshape(b, length, dil * N_PROJ)
    hb = tq // HALF_WINDOW
    nh = length // HALF_WINDOW
    has_prev = o_prev is not None

    def cur(col):
        return pl.BlockSpec((1, tq, 1024), lambda bi, r, i: (bi, i, r * ncol + col))

    def prv(col):
        return pl.BlockSpec((1, HALF_WINDOW, 1024),
                            lambda bi, r, i: (bi, jnp.maximum(i * hb - 1, 0), r * ncol + col))

    def nxt(col):
        return pl.BlockSpec((1, HALF_WINDOW, 1024),
                            lambda bi, r, i: (bi, jnp.minimum((i + 1) * hb, nh - 1), r * ncol + col))

    o_spec = pl.BlockSpec((1, tq, 1024), lambda bi, r, i: (bi, i, r))
    l_spec = pl.BlockSpec((1, tq, 128), lambda bi, r, i: (bi, i, r))
    in_specs = [cur(qc), prv(kc), cur(kc), nxt(kc), prv(vc), cur(vc), nxt(vc)]
    args = [pview] * 7
    if has_prev:
        in_specs += [o_spec, l_spec]
        args += [o_prev.reshape(b, length, dil * 1024), lse_prev.reshape(b, length, dil * 128)]
    out_specs = [o_spec]
    out_shape = [jax.ShapeDtypeStruct((b, length, dil * 1024), BF16)]
    if has_lse:
        out_specs.append(l_spec)
        out_shape.append(jax.ShapeDtypeStruct((b, length, dil * 128), F32))
    span = tq + 2 * HALF_WINDOW
    res = pl.pallas_call(
        functools.partial(_attn_kernel, tq=tq, has_prev=has_prev, has_lse=has_lse),
        grid=(b, dil, length // tq),
        in_specs=in_specs,
        out_specs=out_specs,
        out_shape=out_shape,
        scratch_shapes=[pltpu.VMEM((span, 1024), BF16), pltpu.VMEM((span, 1024), BF16)],
        compiler_params=_cparams(("parallel", "parallel", "parallel")),
        name=f"attn_d{dil}",
    )(*args)
    o = res[0].reshape(b, s, 1024)
    lse = res[1].reshape(b, s, 128) if has_lse else None
    return o, lse


def _dilated_attention(proj3, tq):
    o, lse = None, None
    for idx, dil in enumerate(DILATIONS):
        o, lse = _attn_pattern(proj3, dil, o, lse, idx < len(DILATIONS) - 1, tq)
    return o


def _mixout_kernel(x_ref, bg_ref, cg_ref, xh_ref, cgp_ref, xhp_ref, cgn_ref, xhn_ref, cw_ref,
                   att_ref, yf_ref, yb_ref, xs_ref, z_ref, ga_ref, gb_ref, gc_ref,
                   d_ref, nw_ref, wa_ref, wb_ref, wc_ref, wo_ref, o_ref):
    i = pl.program_id(1)
    n = pl.num_programs(1)
    cx = cg_ref[0].astype(F32) * xh_ref[0].astype(F32)
    prev = jnp.where(i > 0, cgp_ref[0].astype(F32) * xhp_ref[0].astype(F32), 0.0)
    nxt = jnp.where(i < n - 1, cgn_ref[0].astype(F32) * xhn_ref[0].astype(F32), 0.0)
    cw = cw_ref[...]
    k = cw.shape[0]
    conv = None
    for tap in range(k):
        term = _shift_rows(cx, prev, nxt, tap - k // 2) * cw[tap:tap + 1]
        conv = term if conv is None else conv + term
    ya = _dot((bg_ref[0].astype(F32) * conv).astype(BF16), wa_ref[...])
    yb = _dot(att_ref[0], wb_ref[...])
    y = yf_ref[0].astype(F32) + yb_ref[0].astype(F32) + xs_ref[0].astype(F32) * d_ref[...]
    z = z_ref[0].astype(F32)
    y = y * (z * jax.nn.sigmoid(z))
    gw = SSD_WIDTH // SSD_GROUPS
    parts = []
    for g in range(SSD_GROUPS):
        yg = y[:, g * gw:(g + 1) * gw]
        parts.append(yg * lax.rsqrt(jnp.mean(yg * yg, axis=-1, keepdims=True) + EPS))
    yn = jnp.concatenate(parts, axis=1) * nw_ref[...]
    yc = _dot(yn.astype(BF16), wc_ref[...])
    mix = ga_ref[0].astype(F32) * ya + gb_ref[0].astype(F32) * yb + gc_ref[0].astype(F32) * yc
    o_ref[0] = x_ref[0] + _dot(mix.astype(BF16), wo_ref[...])


def _mixout(x3, proj3, att, yf, yb, xbc_c, conv_a_w, d_exp, norm_w, wa, wb, wc, wo, ts):
    b, s, _ = x3.shape
    hb = ts // HALO
    nhalo = s // HALO

    def cur(width, col):
        return pl.BlockSpec((1, ts, width), lambda bi, i: (bi, i, col))

    def prv(col):
        return pl.BlockSpec((1, HALO, 1024), lambda bi, i: (bi, jnp.maximum(i * hb - 1, 0), col))

    def nxt(col):
        return pl.BlockSpec((1, HALO, 1024), lambda bi, i: (bi, jnp.minimum((i + 1) * hb, nhalo - 1), col))

    def whole(shape):
        return pl.BlockSpec(shape, lambda bi, i: (0, 0), pipeline_mode=pl.Buffered(1))

    ca = COL_A // 1024
    cgate = COL_GATE // 1024
    in_specs = [
        cur(1024, 0),
        cur(1024, ca), cur(1024, ca + 1), cur(1024, ca + 2),
        prv(ca + 1), prv(ca + 2), nxt(ca + 1), nxt(ca + 2),
        whole(conv_a_w.shape),
        cur(1024, 0),
        cur(SSD_WIDTH, 0), cur(SSD_WIDTH, 0),
        cur(SSD_WIDTH, 0),
        cur(SSD_WIDTH, COL_Z // SSD_WIDTH),
        cur(1024, cgate), cur(1024, cgate + 1), cur(1024, cgate + 2),
        whole((1, SSD_WIDTH)), whole((1, SSD_WIDTH)),
        whole(wa.shape), whole(wb.shape), whole(wc.shape), whole(wo.shape),
    ]
    args = [x3, proj3, proj3, proj3, proj3, proj3, proj3, proj3, conv_a_w,
            att, yf, yb, xbc_c, proj3, proj3, proj3, proj3, d_exp, norm_w, wa, wb, wc, wo]
    return pl.pallas_call(
        _mixout_kernel,
        grid=(b, s // ts),
        in_specs=in_specs,
        out_specs=cur(1024, 0),
        out_shape=jax.ShapeDtypeStruct((b, s, D_MODEL), F32),
        compiler_params=_cparams(("parallel", "parallel")),
        name="mixout",
    )(*args)


def _rms(x, w):
    return x * lax.rsqrt(jnp.mean(x * x, axis=-1, keepdims=True) + EPS) * w


def _mlp_kernel(x_ref, nw_ref, w1_ref, w2_ref, fw_ref, o_ref, *, final_norm):
    x = x_ref[...]
    h = _rms(x, nw_ref[...]).astype(BF16)
    a = jnp.maximum(_dot(h, w1_ref[...]), 0.0)
    y = x + _dot((a * a).astype(BF16), w2_ref[...])
    if final_norm:
        y = _rms(y, fw_ref[...])
    o_ref[...] = y


def _mlp(x2, nw, w1, w2, fw, final_norm, tm):
    t = x2.shape[0]
    row = lambda i: (i, 0)

    def whole(shape):
        return pl.BlockSpec(shape, lambda i: (0, 0), pipeline_mode=pl.Buffered(1))

    return pl.pallas_call(
        functools.partial(_mlp_kernel, final_norm=final_norm),
        grid=(t // tm,),
        in_specs=[pl.BlockSpec((tm, D_MODEL), row), whole((1, D_MODEL)), whole(w1.shape), whole(w2.shape),
                  whole((1, D_MODEL))],
        out_specs=pl.BlockSpec((tm, D_MODEL), row),
        out_shape=jax.ShapeDtypeStruct((t, D_MODEL), F32),
        compiler_params=_cparams(("parallel",)),
        name="mlp",
    )(x2, nw, w1, w2, fw)


def _rope_tables(positions):
    half = ROT_DIM // 2
    inv_freq = jnp.power(ROPE_THETA, -jnp.arange(0, ROT_DIM, 2, dtype=F32) / ROT_DIM)
    ang = positions.astype(F32)[..., None] * inv_freq
    cos, sin = jnp.cos(ang), jnp.sin(ang)
    shp = ang.shape[:-1]
    ones = jnp.ones(shp + (HEAD_DIM - ROT_DIM,), F32)
    zer = lambda k: jnp.zeros(shp + (k,), F32)
    c64 = jnp.concatenate([cos, cos, ones], -1)
    sa64 = jnp.concatenate([-sin, zer(HEAD_DIM - half)], -1)
    sb64 = jnp.concatenate([zer(half), sin, zer(HEAD_DIM - ROT_DIM)], -1)
    rep = lambda v: jnp.concatenate([v, v], -1).reshape(-1, 128)
    return rep(c64), rep(sa64), rep(sb64)


def _head_expand(offset):
    r = lax.broadcasted_iota(jnp.int32, (128, SSD_WIDTH), 0)
    c = lax.broadcasted_iota(jnp.int32, (128, SSD_WIDTH), 1)
    return jnp.where(r == c // SSD_HEAD_DIM + offset, 1.0, 0.0).astype(BF16)


def _pad128(v):
    return jnp.pad(v.reshape(1, -1), ((0, 0), (0, 128 - v.size)))


def _layer(x3, tables, p, final_w, tiles):
    b, s, _ = x3.shape
    x2 = x3.reshape(b * s, D_MODEL)
    w_in = p["w_in"]
    dt0 = COL_GATE
    w_main = jnp.concatenate([w_in[:, :dt0], w_in[:, dt0 + 2 * SSD_HEADS:]], axis=1).astype(BF16)
    w_dt = jnp.pad(w_in[:, dt0:dt0 + 2 * SSD_HEADS], ((0, 0), (0, 128 - 2 * SSD_HEADS)))
    wd_hi = w_dt.astype(BF16)
    wd_lo = (w_dt - wd_hi.astype(F32)).astype(BF16)
    proj, dt_raw = _inproj(x2, p["mix_norm_w"].reshape(1, -1), w_main, wd_hi, wd_lo, *tables, tiles["proj_tm"])
    proj3 = proj.reshape(b, s, N_PROJ)

    xbc_c = _ssdconv(proj3, p["ssd_conv_w"], p["ssd_conv_b"].reshape(1, -1), tiles["conv_ts"])
    bias128 = _pad128(p["ssd_dt_bias"])
    a128 = _pad128(-jnp.exp(p["ssd_a_log"]))
    yf, yb = _ssdscan(xbc_c, dt_raw.reshape(b, s, 128), bias128, a128, _head_expand(0), _head_expand(SSD_HEADS))

    att = _dilated_attention(proj3, tiles["attn_tq"])

    d_exp = jnp.repeat(p["ssd_d"], SSD_HEAD_DIM).reshape(1, -1)
    x_mid = _mixout(x3, proj3, att, yf, yb, xbc_c, p["conv_a_w"], d_exp, p["ssd_norm_w"].reshape(1, -1),
                    p["w_a_out"].astype(BF16), p["w_b_out"].astype(BF16), p["w_c_out"].astype(BF16),
                    p["w_o"].astype(BF16), tiles["mix_ts"])
    fw = final_w if final_w is not None else p["mlp_norm_w"]
    out = _mlp(x_mid.reshape(b * s, D_MODEL), p["mlp_norm_w"].reshape(1, -1), p["w_ff1"].astype(BF16),
               p["w_ff2"].astype(BF16), fw.reshape(1, -1), final_w is not None, tiles["mlp_tm"])
    return out.reshape(b, s, D_MODEL)


_NAMES = ("mix_norm_w", "w_in", "conv_a_w", "w_a_out", "w_b_out", "ssd_conv_w", "ssd_conv_b", "ssd_a_log",
          "ssd_dt_bias", "ssd_d", "ssd_norm_w", "w_c_out", "w_o", "mlp_norm_w", "w_ff1", "w_ff2")


def _tiles(s):
    return {"proj_tm": min(1024, s), "conv_ts": min(512, s), "attn_tq": 128, "mix_ts": min(256, s),
            "mlp_tm": min(512, s)}


def kernel(x, positions, mix_norm_w, w_in, conv_a_w, w_a_out, w_b_out, ssd_conv_w, ssd_conv_b, ssd_a_log,
           ssd_dt_bias, ssd_d, ssd_norm_w, w_c_out, w_o, mlp_norm_w, w_ff1, w_ff2, final_norm_w):
    stacked = (mix_norm_w, w_in, conv_a_w, w_a_out, w_b_out, ssd_conv_w, ssd_conv_b, ssd_a_log,
               ssd_dt_bias, ssd_d, ssd_norm_w, w_c_out, w_o, mlp_norm_w, w_ff1, w_ff2)
    depth = w_in.shape[0]
    tables = _rope_tables(positions)
    tiles = _tiles(x.shape[1])
    for l in range(depth):
        p = {name: arr[l] for name, arr in zip(_NAMES, stacked)}
        x = _layer(x, tables, p, final_norm_w if l == depth - 1 else None, tiles)
    return x
```

```python
import functools

import jax
import jax.numpy as jnp
from jax import lax
from jax.experimental import pallas as pl
from jax.experimental.pallas import tpu as pltpu

F32 = jnp.float32
BF16 = jnp.bfloat16

D_MODEL = 1024
EPS = 1e-6
NEG_BIG = -1e30
N_HEADS = 16
HEAD_DIM = 64
ROT_DIM = 16
ROPE_THETA = 500000.0
DILATIONS = (1, 4, 16)
HALF_WINDOW = 64
SSD_WIDTH = 2048
SSD_HEADS = 32
SSD_HEAD_DIM = 64
SSD_GROUPS = 4
SSD_STATE = 128
SSD_CHUNK = 128
SSD_CONV_DIM = SSD_WIDTH + 2 * SSD_GROUPS * SSD_STATE
D_FF = 4 * D_MODEL

N_PROJ = 11264
COL_A = 0
COL_GATE = 3072
COL_Z = 6144
COL_XBC = 8192
N_QKV = 3072
PROJ_TN = 512

VMEM_LIMIT = 56 * 1024 * 1024
HALO = 16


def _cparams(sem):
    return pltpu.CompilerParams(dimension_semantics=sem, vmem_limit_bytes=VMEM_LIMIT)


def _dot(a, b):
    return jnp.dot(a, b, preferred_element_type=F32)


def _dot_nt(a, b):
    return lax.dot_general(a, b, (((1,), (1,)), ((), ())), preferred_element_type=F32)


def _split2(v):
    hi = v.astype(BF16)
    lo = (v - hi.astype(F32)).astype(BF16)
    return hi, lo


def _split3(v):
    hi = v.astype(BF16)
    r = v - hi.astype(F32)
    mid = r.astype(BF16)
    lo = (r - mid.astype(F32)).astype(BF16)
    return hi, mid, lo


def _inproj_kernel(x_ref, nw_ref, w_ref, wdh_ref, wdl_ref, cos_ref, sa_ref, sb_ref,
                   proj_ref, qkv_ref, dt_ref, u_ref):
    j = pl.program_id(1)

    @pl.when(j == 0)
    def _():
        x = x_ref[...]
        u = x * lax.rsqrt(jnp.mean(x * x, axis=-1, keepdims=True) + EPS) * nw_ref[...]
        uh, ul = _split2(u)
        u_ref[...] = uh
        dt_ref[...] = _dot(uh, wdh_ref[...]) + _dot(uh, wdl_ref[...]) + _dot(ul, wdh_ref[...])

    acc = _dot(u_ref[...], w_ref[...])
    n_main = N_PROJ // PROJ_TN
    g_lo, g_hi = COL_GATE // PROJ_TN, (COL_GATE + 3072) // PROJ_TN
    k_lo, v_lo = n_main + 1024 // PROJ_TN, n_main + 2048 // PROJ_TN
    is_gate = (j >= g_lo) & (j < g_hi)
    is_rot = (j >= n_main) & (j < v_lo)

    @pl.when(is_rot)
    def _():
        reps = PROJ_TN // 128
        c = jnp.tile(cos_ref[...], (1, reps))
        sa = jnp.tile(sa_ref[...], (1, reps))
        sb = jnp.tile(sb_ref[...], (1, reps))
        r = acc * c + pltpu.roll(acc, PROJ_TN - ROT_DIM // 2, 1) * sa + pltpu.roll(acc, ROT_DIM // 2, 1) * sb
        scale = jnp.where(j < k_lo, HEAD_DIM ** -0.5, 1.0).astype(F32)
        qkv_ref[...] = r * scale

    @pl.when(j >= v_lo)
    def _():
        qkv_ref[...] = acc

    @pl.when(is_gate)
    def _():
        proj_ref[...] = jax.nn.sigmoid(acc).astype(BF16)

    @pl.when((j < n_main) & jnp.logical_not(is_gate))
    def _():
        proj_ref[...] = acc.astype(BF16)


def _inproj(x2, nw, w_main, wd_hi, wd_lo, cos_t, sa_t, sb_t, tm):
    t = x2.shape[0]
    n_main = N_PROJ // PROJ_TN
    grid = (t // tm, (N_PROJ + N_QKV) // PROJ_TN)
    row = lambda i, j: (i, 0)
    return pl.pallas_call(
        _inproj_kernel,
        grid=grid,
        in_specs=[
            pl.BlockSpec((tm, D_MODEL), row),
            pl.BlockSpec((1, D_MODEL), lambda i, j: (0, 0)),
            pl.BlockSpec((D_MODEL, PROJ_TN), lambda i, j: (0, j)),
            pl.BlockSpec((D_MODEL, 128), lambda i, j: (0, 0)),
            pl.BlockSpec((D_MODEL, 128), lambda i, j: (0, 0)),
            pl.BlockSpec((tm, 128), row),
            pl.BlockSpec((tm, 128), row),
            pl.BlockSpec((tm, 128), row),
        ],
        out_specs=[
            pl.BlockSpec((tm, PROJ_TN), lambda i, j: (i, jnp.minimum(j, n_main - 1))),
            pl.BlockSpec((tm, PROJ_TN), lambda i, j: (i, jnp.maximum(j - n_main, 0))),
            pl.BlockSpec((tm, 128), row),
        ],
        out_shape=[
            jax.ShapeDtypeStruct((t, N_PROJ), BF16),
            jax.ShapeDtypeStruct((t, N_QKV), F32),
            jax.ShapeDtypeStruct((t, 128), F32),
        ],
        scratch_shapes=[pltpu.VMEM((tm, D_MODEL), BF16)],
        compiler_params=_cparams(("parallel", "arbitrary")),
        name="inproj",
    )(x2, nw, w_main, wd_hi, wd_lo, cos_t, sa_t, sb_t)


def _shift_rows(x, prev, nxt, off):
    ts = x.shape[0]
    if off == 0:
        return x
    row = lax.broadcasted_iota(jnp.int32, x.shape, 0)
    y = pltpu.roll(x, (-off) % ts, 0)
    if off < 0:
        k = -off
        for r in range(k):
            src = prev.shape[0] - k + r
            y = jnp.where(row == r, prev[src:src + 1], y)
    else:
        for r in range(off):
            y = jnp.where(row == ts - off + r, nxt[r:r + 1], y)
    return y


def _ssdconv_kernel(xc_ref, xp_ref, xn_ref, w_ref, b_ref, o_ref):
    i = pl.program_id(1)
    n = pl.num_programs(1)
    x = xc_ref[0].astype(F32)
    prev = jnp.where(i > 0, xp_ref[0].astype(F32), 0.0)
    nxt = jnp.where(i < n - 1, xn_ref[0].astype(F32), 0.0)
    w = w_ref[...]
    k = w.shape[0]
    y = b_ref[...]
    for tap in range(k):
        y = y + _shift_rows(x, prev, nxt, tap - k // 2) * w[tap:tap + 1]
    o_ref[0] = (y * jax.nn.sigmoid(y)).astype(BF16)


def _ssdconv(proj3, conv_w, conv_b, ts):
    b, s, _ = proj3.shape
    nblk = SSD_CONV_DIM // 1024
    col0 = COL_XBC // 1024
    hb = ts // HALO
    nhalo = s // HALO
    return pl.pallas_call(
        _ssdconv_kernel,
        grid=(b, s // ts, nblk),
        in_specs=[
            pl.BlockSpec((1, ts, 1024), lambda bi, i, c: (bi, i, col0 + c)),
            pl.BlockSpec((1, HALO, 1024), lambda bi, i, c: (bi, jnp.maximum(i * hb - 1, 0), col0 + c)),
            pl.BlockSpec((1, HALO, 1024), lambda bi, i, c: (bi, jnp.minimum((i + 1) * hb, nhalo - 1), col0 + c)),
            pl.BlockSpec((conv_w.shape[0], 1024), lambda bi, i, c: (0, c)),
            pl.BlockSpec((1, 1024), lambda bi, i, c: (0, c)),
        ],
        out_specs=pl.BlockSpec((1, ts, 1024), lambda bi, i, c: (bi, i, c)),
        out_shape=jax.ShapeDtypeStruct((b, s, SSD_CONV_DIM), BF16),
        compiler_params=_cparams(("parallel", "parallel", "parallel")),
        name="ssdconv",
    )(proj3, proj3, proj3, conv_w, conv_b)


def _softplus(v):
    return jnp.maximum(v, 0.0) + jnp.log(1.0 + jnp.exp(-jnp.abs(v)))


def _ssd_direction(xbc_ref, dtraw, bias, a, expand, lane_off, reverse, state_ref, out_ref):
    t = SSD_CHUNK
    li = lax.broadcasted_iota(jnp.int32, (t, t), 0)
    si = lax.broadcasted_iota(jnp.int32, (t, t), 1)
    keep = (si >= li) if reverse else (si <= li)
    tri = jnp.where(keep, 1.0, 0.0).astype(BF16)

    dt = _softplus(dtraw + bias)
    la = dt * a
    h3 = _split3(la)
    cum = _dot(tri, h3[0]) + _dot(tri, h3[1]) + _dot(tri, h3[2])
    cum_t = cum.T
    edge = cum[0:1, :] if reverse else cum[t - 1:t, :]
    ecum = jnp.exp(cum)
    wst = jnp.exp(edge - cum)

    def expand2(v):
        hi, lo = _split2(v)
        return _dot(hi, expand) + _dot(lo, expand)

    dtx = expand2(dt)
    ecx = expand2(ecum)
    wsx = expand2(wst)
    xd = xbc_ref[0, :, 0:SSD_WIDTH].astype(F32) * dtx
    xdb = xd.astype(BF16)
    xw = (xd * wsx).astype(BF16)
    edge_row = t - 1 if not reverse else 0
    chunk_decay = ecx[edge_row:edge_row + 1, :]

    lane = lax.broadcasted_iota(jnp.int32, (t, 128), 1)
    gw = SSD_WIDTH // SSD_GROUPS
    for g in range(SSD_GROUPS):
        bg = xbc_ref[0, :, SSD_WIDTH + g * SSD_STATE:SSD_WIDTH + (g + 1) * SSD_STATE]
        cg = xbc_ref[0, :, SSD_WIDTH + (SSD_GROUPS + g) * SSD_STATE:SSD_WIDTH + (SSD_GROUPS + g + 1) * SSD_STATE]
        cb = _dot_nt(cg, bg)
        st = state_ref[g]
        yoff = _dot(cg, st.astype(BF16)) * ecx[:, g * gw:(g + 1) * gw]
        for pair in range(gw // 128):
            h0 = g * (gw // SSD_HEAD_DIM) + 2 * pair
            c0 = h0 * SSD_HEAD_DIM
            xdp = xdb[:, c0:c0 + 128]
            acc = yoff[:, pair * 128:(pair + 1) * 128]
            for sub in range(2):
                hl = lane_off + h0 + sub
                seg = cum[:, hl:hl + 1] - cum_t[hl:hl + 1, :]
                decay = jnp.exp(jnp.where(keep, seg, -jnp.inf))
                m = (cb * decay).astype(BF16)
                half = (lane < 64) if sub == 0 else (lane >= 64)
                acc = acc + _dot(m, jnp.where(half, xdp, jnp.zeros_like(xdp)))
            out_ref[0, :, c0:c0 + 128] = acc.astype(BF16)
        bg_t = bg.astype(F32).T.astype(BF16)
        state_ref[g] = st * chunk_decay[:, g * gw:(g + 1) * gw] + _dot(bg_t, xw[:, g * gw:(g + 1) * gw])


def _ssdscan_kernel(xf_ref, xb_ref, dtf_ref, dtb_ref, bias_ref, a_ref, ef_ref, eb_ref,
                    yf_ref, yb_ref, sf_ref, sb_ref):
    @pl.when(pl.program_id(1) == 0)
    def _():
        sf_ref[...] = jnp.zeros_like(sf_ref)
        sb_ref[...] = jnp.zeros_like(sb_ref)

    bias = bias_ref[...]
    a = a_ref[...]
    _ssd_direction(xf_ref, dtf_ref[0], bias, a, ef_ref[...], 0, False, sf_ref, yf_ref)
    _ssd_direction(xb_ref, dtb_ref[0], bias, a, eb_ref[...], SSD_HEADS, True, sb_ref, yb_ref)


def _ssdscan(xbc_c, dt_raw3, bias128, a128, exp_f, exp_b):
    b, s, _ = xbc_c.shape
    nc = s // SSD_CHUNK
    fwd = lambda bi, c: (bi, c, 0)
    bwd = lambda bi, c: (bi, nc - 1 - c, 0)
    const = lambda bi, c: (0, 0)
    state = pltpu.VMEM((SSD_GROUPS, SSD_STATE, SSD_WIDTH // SSD_GROUPS), F32)
    return pl.pallas_call(
        _ssdscan_kernel,
        grid=(b, nc),
        in_specs=[
            pl.BlockSpec((1, SSD_CHUNK, SSD_CONV_DIM), fwd),
            pl.BlockSpec((1, SSD_CHUNK, SSD_CONV_DIM), bwd),
            pl.BlockSpec((1, SSD_CHUNK, 128), fwd),
            pl.BlockSpec((1, SSD_CHUNK, 128), bwd),
            pl.BlockSpec((1, 128), const),
            pl.BlockSpec((1, 128), const),
            pl.BlockSpec((128, SSD_WIDTH), const),
            pl.BlockSpec((128, SSD_WIDTH), const),
        ],
        out_specs=[
            pl.BlockSpec((1, SSD_CHUNK, SSD_WIDTH), fwd),
            pl.BlockSpec((1, SSD_CHUNK, SSD_WIDTH), bwd),
        ],
        out_shape=[
            jax.ShapeDtypeStruct((b, s, SSD_WIDTH), BF16),
            jax.ShapeDtypeStruct((b, s, SSD_WIDTH), BF16),
        ],
        scratch_shapes=[state, state],
        compiler_params=_cparams(("parallel", "arbitrary")),
        name="ssdscan",
    )(xbc_c, xbc_c, dt_raw3, dt_raw3, bias128, a128, exp_f, exp_b)


ATT_TQ = 128
ATT_W = ATT_TQ + 2 * HALF_WINDOW


def _attn_kernel(q_ref, k_ref, v_ref, o_ref, bias_ref, m_ref, l_ref, acc_ref, *, seq):
    tq, w, hw = ATT_TQ, ATT_W, HALF_WINDOW
    ql = lax.broadcasted_iota(jnp.int32, (tq, w), 0)
    kj = lax.broadcasted_iota(jnp.int32, (tq, w), 1)
    for v in range(3):
        bias_ref[v] = jnp.where(jnp.abs(kj - ql - hw * v) <= hw, 0.0, NEG_BIG).astype(F32)
    lane = lax.broadcasted_iota(jnp.int32, (tq, 128), 1)
    head0 = lane < HEAD_DIM

    order = tuple(reversed(DILATIONS))
    for pi, d in enumerate(order):
        length = seq // d
        nblk = length // tq
        shift = nblk.bit_length() - 1
        first, last = pi == 0, pi == len(order) - 1

        def rows(start, size, d=d):
            return pl.ds(start, size) if d == 1 else pl.ds(start, size, stride=d)

        def body(idx, carry, d=d, length=length, nblk=nblk, shift=shift, first=first, last=last, rows=rows):
            r = lax.shift_right_logical(idx, shift)
            i = lax.bitwise_and(idx, nblk - 1)
            ksub = jnp.clip(i * tq - hw, 0, length - w)
            var = lax.shift_right_logical(i * tq - ksub, hw.bit_length() - 1)
            qstart = r + d * (i * tq)
            kstart = r + d * ksub
            if d == 1:
                qstart = pl.multiple_of(qstart, tq)
                kstart = pl.multiple_of(kstart, hw)
            q2 = q_ref[0, rows(qstart, tq), :]
            k2 = k_ref[0, rows(kstart, w), :].astype(BF16)
            v2 = v_ref[0, rows(kstart, w), :].astype(BF16)
            bias = bias_ref[var]
            zero = jnp.zeros_like(q2)
            accs, ms, ls = [], [], []
            for h in range(2):
                sel = head0 if h == 0 else jnp.logical_not(head0)
                qh = jnp.where(sel, q2, zero).astype(BF16)
                s = _dot_nt(qh, k2) + bias
                m = jnp.max(s, axis=-1, keepdims=True)
                p = jnp.exp(s - m)
                ls.append(jnp.sum(p, axis=-1, keepdims=True))
                ms.append(m)
                accs.append(_dot(p.astype(BF16), v2))
            acc = jnp.where(head0, accs[0], accs[1])
            m = jnp.where(head0, ms[0], ms[1])
            den = jnp.where(head0, ls[0], ls[1])
            if not first:
                mo = m_ref[rows(qstart, tq), :]
                mn = jnp.maximum(mo, m)
                eo = jnp.exp(mo - mn)
                eb = jnp.exp(m - mn)
                den = l_ref[rows(qstart, tq), :] * eo + den * eb
                acc = acc_ref[rows(qstart, tq), :] * eo + acc * eb
                m = mn
            if last:
                o_ref[0, rows(qstart, tq), :] = (acc / den).astype(BF16)
            else:
                m_ref[rows(qstart, tq), :] = m
                l_ref[rows(qstart, tq), :] = den
                acc_ref[rows(qstart, tq), :] = acc
            return carry

        lax.fori_loop(0, d * nblk, body, 0, unroll=8)


def _dilated_attention(qkv3):
    b, s, _ = qkv3.shape
    assert s // max(DILATIONS) >= ATT_W and s % (max(DILATIONS) * ATT_TQ) == 0
    npair = N_HEADS // 2
    blk = lambda part: pl.BlockSpec((1, s, 128), lambda bi, hp: (bi, 0, part * npair + hp))
    seq_f32 = pltpu.VMEM((s, 128), F32)
    return pl.pallas_call(
        functools.partial(_attn_kernel, seq=s),
        grid=(b, npair),
        in_specs=[blk(0), blk(1), blk(2)],
        out_specs=blk(0),
        out_shape=jax.ShapeDtypeStruct((b, s, N_HEADS * HEAD_DIM), BF16),
        scratch_shapes=[pltpu.VMEM((3, ATT_TQ, ATT_W), F32), seq_f32, seq_f32, seq_f32],
        compiler_params=_cparams(("parallel", "parallel")),
        name="attn",
    )(qkv3, qkv3, qkv3)


def _mixout_kernel(x_ref, bg_ref, cg_ref, xh_ref, cgp_ref, xhp_ref, cgn_ref, xhn_ref, cw_ref,
                   att_ref, yf_ref, yb_ref, xs_ref, z_ref, ga_ref, gb_ref, gc_ref,
                   d_ref, nw_ref, wa_ref, wb_ref, wc_ref, wo_ref, o_ref):
    i = pl.program_id(1)
    n = pl.num_programs(1)
    cx = cg_ref[0].astype(F32) * xh_ref[0].astype(F32)
    prev = jnp.where(i > 0, cgp_ref[0].astype(F32) * xhp_ref[0].astype(F32), 0.0)
    nxt = jnp.where(i < n - 1, cgn_ref[0].astype(F32) * xhn_ref[0].astype(F32), 0.0)
    cw = cw_ref[...]
    k = cw.shape[0]
    conv = None
    for tap in range(k):
        term = _shift_rows(cx, prev, nxt, tap - k // 2) * cw[tap:tap + 1]
        conv = term if conv is None else conv + term
    ya = _dot((bg_ref[0].astype(F32) * conv).astype(BF16), wa_ref[...])
    yb = _dot(att_ref[0], wb_ref[...])
    y = yf_ref[0].astype(F32) + yb_ref[0].astype(F32) + xs_ref[0].astype(F32) * d_ref[...]
    z = z_ref[0].astype(F32)
    y = y * (z * jax.nn.sigmoid(z))
    gw = SSD_WIDTH // SSD_GROUPS
    parts = []
    for g in range(SSD_GROUPS):
        yg = y[:, g * gw:(g + 1) * gw]
        parts.append(yg * lax.rsqrt(jnp.mean(yg * yg, axis=-1, keepdims=True) + EPS))
    yn = jnp.concatenate(parts, axis=1) * nw_ref[...]
    yc = _dot(yn.astype(BF16), wc_ref[...])
    mix = ga_ref[0].astype(F32) * ya + gb_ref[0].astype(F32) * yb + gc_ref[0].astype(F32) * yc
    o_ref[0] = x_ref[0] + _dot(mix.astype(BF16), wo_ref[...])


def _mixout(x3, proj3, att, yf, yb, xbc_c, conv_a_w, d_exp, norm_w, wa, wb, wc, wo, ts):
    b, s, _ = x3.shape
    hb = ts // HALO
    nhalo = s // HALO

    def cur(width, col):
        return pl.BlockSpec((1, ts, width), lambda bi, i: (bi, i, col))

    def prv(col):
        return pl.BlockSpec((1, HALO, 1024), lambda bi, i: (bi, jnp.maximum(i * hb - 1, 0), col))

    def nxt(col):
        return pl.BlockSpec((1, HALO, 1024), lambda bi, i: (bi, jnp.minimum((i + 1) * hb, nhalo - 1), col))

    def whole(shape):
        return pl.BlockSpec(shape, lambda bi, i: (0, 0), pipeline_mode=pl.Buffered(1))

    ca = COL_A // 1024
    cgate = COL_GATE // 1024
    in_specs = [
        cur(1024, 0),
        cur(1024, ca), cur(1024, ca + 1), cur(1024, ca + 2),
        prv(ca + 1), prv(ca + 2), nxt(ca + 1), nxt(ca + 2),
        whole(conv_a_w.shape),
        cur(1024, 0),
        cur(SSD_WIDTH, 0), cur(SSD_WIDTH, 0),
        cur(SSD_WIDTH, 0),
        cur(SSD_WIDTH, COL_Z // SSD_WIDTH),
        cur(1024, cgate), cur(1024, cgate + 1), cur(1024, cgate + 2),
        whole((1, SSD_WIDTH)), whole((1, SSD_WIDTH)),
        whole(wa.shape), whole(wb.shape), whole(wc.shape), whole(wo.shape),
    ]
    args = [x3, proj3, proj3, proj3, proj3, proj3, proj3, proj3, conv_a_w,
            att, yf, yb, xbc_c, proj3, proj3, proj3, proj3, d_exp, norm_w, wa, wb, wc, wo]
    return pl.pallas_call(
        _mixout_kernel,
        grid=(b, s // ts),
        in_specs=in_specs,
        out_specs=cur(1024, 0),
        out_shape=jax.ShapeDtypeStruct((b, s, D_MODEL), F32),
        compiler_params=_cparams(("parallel", "parallel")),
        name="mixout",
    )(*args)


def _rms(x, w):
    return x * lax.rsqrt(jnp.mean(x * x, axis=-1, keepdims=True) + EPS) * w


def _mlp_kernel(x_ref, nw_ref, w1_ref, w2_ref, fw_ref, o_ref, *, final_norm):
    x = x_ref[...]
    h = _rms(x, nw_ref[...]).astype(BF16)
    a = jnp.maximum(_dot(h, w1_ref[...]), 0.0)
    y = x + _dot((a * a).astype(BF16), w2_ref[...])
    if final_norm:
        y = _rms(y, fw_ref[...])
    o_ref[...] = y


def _mlp(x2, nw, w1, w2, fw, final_norm, tm):
    t = x2.shape[0]
    row = lambda i: (i, 0)

    def whole(shape):
        return pl.BlockSpec(shape, lambda i: (0, 0), pipeline_mode=pl.Buffered(1))

    return pl.pallas_call(
        functools.partial(_mlp_kernel, final_norm=final_norm),
        grid=(t // tm,),
        in_specs=[pl.BlockSpec((tm, D_MODEL), row), whole((1, D_MODEL)), whole(w1.shape), whole(w2.shape),
                  whole((1, D_MODEL))],
        out_specs=pl.BlockSpec((tm, D_MODEL), row),
        out_shape=jax.ShapeDtypeStruct((t, D_MODEL), F32),
        compiler_params=_cparams(("parallel",)),
        name="mlp",
    )(x2, nw, w1, w2, fw)


def _rope_tables(positions):
    half = ROT_DIM // 2
    inv_freq = jnp.power(ROPE_THETA, -jnp.arange(0, ROT_DIM, 2, dtype=F32) / ROT_DIM)
    ang = positions.astype(F32)[..., None] * inv_freq
    cos, sin = jnp.cos(ang), jnp.sin(ang)
    shp = ang.shape[:-1]
    ones = jnp.ones(shp + (HEAD_DIM - ROT_DIM,), F32)
    zer = lambda k: jnp.zeros(shp + (k,), F32)
    c64 = jnp.concatenate([cos, cos, ones], -1)
    sa64 = jnp.concatenate([-sin, zer(HEAD_DIM - half)], -1)
    sb64 = jnp.concatenate([zer(half), sin, zer(HEAD_DIM - ROT_DIM)], -1)
    rep = lambda v: jnp.concatenate([v, v], -1).reshape(-1, 128)
    return rep(c64), rep(sa64), rep(sb64)


def _head_expand(offset):
    r = lax.broadcasted_iota(jnp.int32, (128, SSD_WIDTH), 0)
    c = lax.broadcasted_iota(jnp.int32, (128, SSD_WIDTH), 1)
    return jnp.where(r == c // SSD_HEAD_DIM + offset, 1.0, 0.0).astype(BF16)


def _pad128(v):
    return jnp.pad(v.reshape(1, -1), ((0, 0), (0, 128 - v.size)))


def _layer(x3, tables, p, final_w, tiles):
    b, s, _ = x3.shape
    x2 = x3.reshape(b * s, D_MODEL)
    w_in = p["w_in"]
    o_qkv, o_z, o_xbc = 3072, 6144, 8192
    o_dt = o_xbc + SSD_CONV_DIM
    o_gate = o_dt + 2 * SSD_HEADS
    w_main = jnp.concatenate([w_in[:, :o_qkv], w_in[:, o_gate:], w_in[:, o_z:o_dt], w_in[:, o_qkv:o_z]],
                             axis=1).astype(BF16)
    w_dt = jnp.pad(w_in[:, o_dt:o_gate], ((0, 0), (0, 128 - 2 * SSD_HEADS)))
    wd_hi = w_dt.astype(BF16)
    wd_lo = (w_dt - wd_hi.astype(F32)).astype(BF16)
    proj, qkv, dt_raw = _inproj(x2, p["mix_norm_w"].reshape(1, -1), w_main, wd_hi, wd_lo, *tables,
                                tiles["proj_tm"])
    proj3 = proj.reshape(b, s, N_PROJ)

    xbc_c = _ssdconv(proj3, p["ssd_conv_w"], p["ssd_conv_b"].reshape(1, -1), tiles["conv_ts"])
    bias128 = _pad128(p["ssd_dt_bias"])
    a128 = _pad128(-jnp.exp(p["ssd_a_log"]))
    yf, yb = _ssdscan(xbc_c, dt_raw.reshape(b, s, 128), bias128, a128, _head_expand(0), _head_expand(SSD_HEADS))

    att = _dilated_attention(qkv.reshape(b, s, N_QKV))

    d_exp = jnp.repeat(p["ssd_d"], SSD_HEAD_DIM).reshape(1, -1)
    x_mid = _mixout(x3, proj3, att, yf, yb, xbc_c, p["conv_a_w"], d_exp, p["ssd_norm_w"].reshape(1, -1),
                    p["w_a_out"].astype(BF16), p["w_b_out"].astype(BF16), p["w_c_out"].astype(BF16),
                    p["w_o"].astype(BF16), tiles["mix_ts"])
    fw = final_w if final_w is not None else p["mlp_norm_w"]
    out = _mlp(x_mid.reshape(b * s, D_MODEL), p["mlp_norm_w"].reshape(1, -1), p["w_ff1"].astype(BF16),
               p["w_ff2"].astype(BF16), fw.reshape(1, -1), final_w is not None, tiles["mlp_tm"])
    return out.reshape(b, s, D_MODEL)


_NAMES = ("mix_norm_w", "w_in", "conv_a_w", "w_a_out", "w_b_out", "ssd_conv_w", "ssd_conv_b", "ssd_a_log",
          "ssd_dt_bias", "ssd_d", "ssd_norm_w", "w_c_out", "w_o", "mlp_norm_w", "w_ff1", "w_ff2")


def _tiles(s):
    return {"proj_tm": min(1024, s), "conv_ts": min(512, s), "mix_ts": min(256, s), "mlp_tm": min(512, s)}


def kernel(x, positions, mix_norm_w, w_in, conv_a_w, w_a_out, w_b_out, ssd_conv_w, ssd_conv_b, ssd_a_log,
           ssd_dt_bias, ssd_d, ssd_norm_w, w_c_out, w_o, mlp_norm_w, w_ff1, w_ff2, final_norm_w):
    stacked = (mix_norm_w, w_in, conv_a_w, w_a_out, w_b_out, ssd_conv_w, ssd_conv_b, ssd_a_log,
               ssd_dt_bias, ssd_d, ssd_norm_w, w_c_out, w_o, mlp_norm_w, w_ff1, w_ff2)
    depth = w_in.shape[0]
    tables = _rope_tables(positions)
    tiles = _tiles(x.shape[1])
    for l in range(depth):
        p = {name: arr[l] for name, arr in zip(_NAMES, stacked)}
        x = _layer(x, tables, p, final_norm_w if l == depth - 1 else None, tiles)
    return x
```

```python
import functools
import math

import jax
import jax.numpy as jnp
from jax import lax
from jax.experimental import pallas as pl
from jax.experimental.pallas import tpu as pltpu

F32 = jnp.float32
BF16 = jnp.bfloat16

D_MODEL = 1024
EPS = 1e-6
NEG_BIG = -1e30
LOG2E = math.log2(math.e)
EXP2_CAP = 127.0
N_HEADS = 16
HEAD_DIM = 64
ROT_DIM = 16
ROPE_THETA = 500000.0
DILATIONS = (1, 4, 16)
HALF_WINDOW = 64
SSD_WIDTH = 2048
SSD_HEADS = 32
SSD_HEAD_DIM = 64
SSD_GROUPS = 4
SSD_STATE = 128
SSD_CHUNK = 128
SSD_CONV_DIM = SSD_WIDTH + 2 * SSD_GROUPS * SSD_STATE
D_FF = 4 * D_MODEL

W_A, W_QK, W_V, W_Z, W_XBC = 0, 3072, 5120, 6144, 8192
W_DT = W_XBC + SSD_CONV_DIM
W_GATE = W_DT + 2 * SSD_HEADS

VMEM_LIMIT = 56 * 1024 * 1024
HALO = 16
PAD = 8


def _cparams(sem):
    return pltpu.CompilerParams(dimension_semantics=sem, vmem_limit_bytes=VMEM_LIMIT)


def _dot(a, b):
    return jnp.dot(a, b, preferred_element_type=F32)


def _dot_nt(a, b):
    return lax.dot_general(a, b, (((1,), (1,)), ((), ())), preferred_element_type=F32)


def _split2(v):
    hi = v.astype(BF16)
    lo = (v - hi.astype(F32)).astype(BF16)
    return hi, lo


def _split3(v):
    hi = v.astype(BF16)
    r = v - hi.astype(F32)
    mid = r.astype(BF16)
    lo = (r - mid.astype(F32)).astype(BF16)
    return hi, mid, lo


def _rms(x, w):
    return x * lax.rsqrt(jnp.mean(x * x, axis=-1, keepdims=True) + EPS) * w


def _norm_kernel(x_ref, nw_ref, wdh_ref, wdl_ref, u_ref, dt_ref):
    u = _rms(x_ref[...], nw_ref[...])
    uh, ul = _split2(u)
    u_ref[...] = uh
    dt_ref[...] = _dot(uh, wdh_ref[...]) + _dot(uh, wdl_ref[...]) + _dot(ul, wdh_ref[...])


def _norm(x2, nw, wd_hi, wd_lo, tm):
    t = x2.shape[0]
    row = lambda i: (i, 0)
    const = lambda i: (0, 0)
    return pl.pallas_call(
        _norm_kernel,
        grid=(t // tm,),
        in_specs=[pl.BlockSpec((tm, D_MODEL), row), pl.BlockSpec((1, D_MODEL), const),
                  pl.BlockSpec((D_MODEL, 128), const), pl.BlockSpec((D_MODEL, 128), const)],
        out_specs=[pl.BlockSpec((tm, D_MODEL), row), pl.BlockSpec((tm, 128), row)],
        out_shape=[jax.ShapeDtypeStruct((t, D_MODEL), BF16), jax.ShapeDtypeStruct((t, 128), F32)],
        compiler_params=_cparams(("parallel",)),
        name="norm",
    )(x2, nw, wd_hi, wd_lo)


def _proj_kernel(*refs, kind, tn, q_tiles):
    if kind == "rope":
        u_ref, w_ref, cos_ref, sa_ref, sb_ref, o_ref = refs
    else:
        u_ref, w_ref, o_ref = refs
    acc = _dot(u_ref[...], w_ref[...])
    if kind == "sigmoid":
        acc = jax.nn.sigmoid(acc)
    elif kind == "silu":
        acc = acc * jax.nn.sigmoid(acc)
    elif kind == "rope":
        reps = tn // 128
        c = jnp.tile(cos_ref[...], (1, reps))
        sa = jnp.tile(sa_ref[...], (1, reps))
        sb = jnp.tile(sb_ref[...], (1, reps))
        acc = acc * c + pltpu.roll(acc, tn - ROT_DIM // 2, 1) * sa + pltpu.roll(acc, ROT_DIM // 2, 1) * sb
        acc = acc * jnp.where(pl.program_id(1) < q_tiles, HEAD_DIM ** -0.5 * LOG2E, 1.0).astype(F32)
    o_ref[...] = acc.astype(o_ref.dtype)


def _proj(u, w_bf, col0, width, kind, out_dtype, tm, tn, tables=()):
    t = u.shape[0]
    assert col0 % tn == 0 and width % tn == 0
    j0 = col0 // tn
    row = lambda i, j: (i, 0)
    in_specs = [pl.BlockSpec((tm, D_MODEL), row), pl.BlockSpec((D_MODEL, tn), lambda i, j: (0, j0 + j))]
    in_specs += [pl.BlockSpec((tm, 128), row) for _ in tables]
    return pl.pallas_call(
        functools.partial(_proj_kernel, kind=kind, tn=tn, q_tiles=1024 // tn),
        grid=(t // tm, width // tn),
        in_specs=in_specs,
        out_specs=pl.BlockSpec((tm, tn), lambda i, j: (i, j)),
        out_shape=jax.ShapeDtypeStruct((t, width), out_dtype),
        compiler_params=_cparams(("parallel", "parallel")),
        name=f"proj_{kind}",
    )(u, w_bf, *tables)


def _conv_rows(x, prev, nxt, w, acc):
    ts = x.shape[0]
    xe = jnp.concatenate([prev, x, nxt], axis=0)
    n = xe.shape[0]
    k = w.shape[0]
    for tap in range(k):
        off = tap - k // 2
        sh = xe if off == 0 else pltpu.roll(xe, (-off) % n, 0)
        term = sh[PAD:PAD + ts] * w[tap:tap + 1]
        acc = term if acc is None else acc + term
    return acc


def _ssdconv_kernel(xc_ref, xp_ref, xn_ref, w_ref, b_ref, o_ref):
    i = pl.program_id(1)
    n = pl.num_programs(1)
    x = xc_ref[0].astype(F32)
    prev = jnp.where(i > 0, xp_ref[0].astype(F32)[HALO - PAD:HALO], 0.0)
    nxt = jnp.where(i < n - 1, xn_ref[0].astype(F32)[0:PAD], 0.0)
    y = _conv_rows(x, prev, nxt, w_ref[...], b_ref[...])
    o_ref[0] = (y * jax.nn.sigmoid(y)).astype(BF16)


def _ssdconv(xbc3, conv_w, conv_b, ts):
    b, s, _ = xbc3.shape
    nblk = SSD_CONV_DIM // 1024
    hb = ts // HALO
    nhalo = s // HALO
    return pl.pallas_call(
        _ssdconv_kernel,
        grid=(b, s // ts, nblk),
        in_specs=[
            pl.BlockSpec((1, ts, 1024), lambda bi, i, c: (bi, i, c)),
            pl.BlockSpec((1, HALO, 1024), lambda bi, i, c: (bi, jnp.maximum(i * hb - 1, 0), c)),
            pl.BlockSpec((1, HALO, 1024), lambda bi, i, c: (bi, jnp.minimum((i + 1) * hb, nhalo - 1), c)),
            pl.BlockSpec((conv_w.shape[0], 1024), lambda bi, i, c: (0, c)),
            pl.BlockSpec((1, 1024), lambda bi, i, c: (0, c)),
        ],
        out_specs=pl.BlockSpec((1, ts, 1024), lambda bi, i, c: (bi, i, c)),
        out_shape=jax.ShapeDtypeStruct((b, s, SSD_CONV_DIM), BF16),
        compiler_params=_cparams(("parallel", "parallel", "parallel")),
        name="ssdconv",
    )(xbc3, xbc3, xbc3, conv_w, conv_b)


def _softplus(v):
    return jnp.maximum(v, 0.0) + jnp.log(1.0 + jnp.exp(-jnp.abs(v)))


def _ssd_direction(bi, xbc_ref, dtraw, bias, a, expand, lane_off, reverse, state_ref, out_ref):
    t = SSD_CHUNK
    li = lax.broadcasted_iota(jnp.int32, (t, t), 0)
    si = lax.broadcasted_iota(jnp.int32, (t, t), 1)
    keep = (si >= li) if reverse else (si <= li)
    tri = jnp.where(keep, 1.0, 0.0).astype(BF16)

    dt = _softplus(dtraw + bias)
    la = dt * a
    h3 = _split3(la)
    cum = _dot(tri, h3[0]) + _dot(tri, h3[1]) + _dot(tri, h3[2])
    edge = cum[0:1, :] if reverse else cum[t - 1:t, :]
    cum2 = cum * LOG2E
    src_t = (cum2 - jnp.log(dt) * LOG2E).T
    ecx = _dot(jnp.exp(cum).astype(BF16), expand)
    wsx = _dot((dt * jnp.exp(edge - cum)).astype(BF16), expand)
    xw = xbc_ref[bi, :, 0:SSD_WIDTH] * wsx.astype(BF16)
    edge_row = t - 1 if not reverse else 0
    chunk_decay = ecx[edge_row:edge_row + 1, :]

    lane = lax.broadcasted_iota(jnp.int32, (t, 128), 1)
    low = lane < SSD_HEAD_DIM
    gw = SSD_WIDTH // SSD_GROUPS
    for g in range(SSD_GROUPS):
        bg = xbc_ref[bi, :, SSD_WIDTH + g * SSD_STATE:SSD_WIDTH + (g + 1) * SSD_STATE]
        cg = xbc_ref[bi, :, SSD_WIDTH + (SSD_GROUPS + g) * SSD_STATE:SSD_WIDTH + (SSD_GROUPS + g + 1) * SSD_STATE]
        cb = jnp.where(keep, _dot_nt(cg, bg), 0.0)
        st = state_ref[bi, g]
        yoff = _dot(cg, st.astype(BF16)) * ecx[:, g * gw:(g + 1) * gw]
        for pair in range(gw // 128):
            h0 = g * (gw // SSD_HEAD_DIM) + 2 * pair
            c0 = h0 * SSD_HEAD_DIM
            xp = xbc_ref[bi, :, c0:c0 + 128]
            zero = jnp.zeros_like(xp)
            ms = []
            for sub in range(2):
                hl = lane_off + h0 + sub
                seg = jnp.minimum(cum2[:, hl:hl + 1] - src_t[hl:hl + 1, :], EXP2_CAP)
                ms.append((cb * jnp.exp2(seg)).astype(BF16))
            lhs = jnp.concatenate(ms, axis=1)
            rhs = jnp.concatenate([jnp.where(low, xp, zero), jnp.where(low, zero, xp)], axis=0)
            out_ref[bi, :, c0:c0 + 128] = (yoff[:, pair * 128:(pair + 1) * 128] + _dot(lhs, rhs)).astype(BF16)
        bg_t = bg.astype(F32).T.astype(BF16)
        state_ref[bi, g] = st * chunk_decay[:, g * gw:(g + 1) * gw] + _dot(bg_t, xw[:, g * gw:(g + 1) * gw])


def _ssdscan_kernel(xf_ref, xb_ref, dtf_ref, dtb_ref, bias_ref, a_ref, ef_ref, eb_ref,
                    yf_ref, yb_ref, sf_ref, sb_ref):
    @pl.when(pl.program_id(1) == 0)
    def _():
        sf_ref[...] = jnp.zeros_like(sf_ref)
        sb_ref[...] = jnp.zeros_like(sb_ref)

    bias = bias_ref[...]
    a = a_ref[...]
    for bi in range(xf_ref.shape[0]):
        _ssd_direction(bi, xf_ref, dtf_ref[bi], bias, a, ef_ref[...], 0, False, sf_ref, yf_ref)
        _ssd_direction(bi, xb_ref, dtb_ref[bi], bias, a, eb_ref[...], SSD_HEADS, True, sb_ref, yb_ref)


def _ssdscan(xbc_c, dt_raw3, bias128, a128, exp_f, exp_b, nb):
    b, s, _ = xbc_c.shape
    nc = s // SSD_CHUNK
    fwd = lambda bi, c: (bi, c, 0)
    bwd = lambda bi, c: (bi, nc - 1 - c, 0)
    const = lambda bi, c: (0, 0)
    state = pltpu.VMEM((nb, SSD_GROUPS, SSD_STATE, SSD_WIDTH // SSD_GROUPS), F32)
    return pl.pallas_call(
        _ssdscan_kernel,
        grid=(b // nb, nc),
        in_specs=[
            pl.BlockSpec((nb, SSD_CHUNK, SSD_CONV_DIM), fwd),
            pl.BlockSpec((nb, SSD_CHUNK, SSD_CONV_DIM), bwd),
            pl.BlockSpec((nb, SSD_CHUNK, 128), fwd),
            pl.BlockSpec((nb, SSD_CHUNK, 128), bwd),
            pl.BlockSpec((1, 128), const),
            pl.BlockSpec((1, 128), const),
            pl.BlockSpec((128, SSD_WIDTH), const),
            pl.BlockSpec((128, SSD_WIDTH), const),
        ],
        out_specs=[
            pl.BlockSpec((nb, SSD_CHUNK, SSD_WIDTH), fwd),
            pl.BlockSpec((nb, SSD_CHUNK, SSD_WIDTH), bwd),
        ],
        out_shape=[
            jax.ShapeDtypeStruct((b, s, SSD_WIDTH), BF16),
            jax.ShapeDtypeStruct((b, s, SSD_WIDTH), BF16),
        ],
        scratch_shapes=[state, state],
        compiler_params=_cparams(("parallel", "arbitrary")),
        name="ssdscan",
    )(xbc_c, xbc_c, dt_raw3, dt_raw3, bias128, a128, exp_f, exp_b)


ATT_TQ = 128
ATT_W = ATT_TQ + 2 * HALF_WINDOW


def _attn_kernel(q_ref, k_ref, v_ref, o_ref, bias_ref, m_ref, l_ref, acc_ref, *, seq):
    tq, w, hw = ATT_TQ, ATT_W, HALF_WINDOW
    ql = lax.broadcasted_iota(jnp.int32, (tq, w), 0)
    kj = lax.broadcasted_iota(jnp.int32, (tq, w), 1)
    for v in range(3):
        bias_ref[v] = jnp.where(jnp.abs(kj - ql - hw * v) <= hw, 0.0, NEG_BIG).astype(F32)
    lane = lax.broadcasted_iota(jnp.int32, (tq, 128), 1)
    head0 = lane < HEAD_DIM

    order = tuple(reversed(DILATIONS))
    for pi, d in enumerate(order):
        length = seq // d
        nblk = length // tq
        shift = nblk.bit_length() - 1
        first, last = pi == 0, pi == len(order) - 1

        def rows(start, size, d=d):
            return pl.ds(start, size) if d == 1 else pl.ds(start, size, stride=d)

        def body(idx, carry, d=d, length=length, nblk=nblk, shift=shift, first=first, last=last, rows=rows):
            r = lax.shift_right_logical(idx, shift)
            i = lax.bitwise_and(idx, nblk - 1)
            ksub = jnp.clip(i * tq - hw, 0, length - w)
            var = lax.shift_right_logical(i * tq - ksub, hw.bit_length() - 1)
            qstart = r + d * (i * tq)
            kstart = r + d * ksub
            if d == 1:
                qstart = pl.multiple_of(qstart, tq)
                kstart = pl.multiple_of(kstart, hw)
            q2 = q_ref[0, rows(qstart, tq), :]
            k2 = k_ref[0, rows(kstart, w), :].astype(BF16)
            v2 = v_ref[0, rows(kstart, w), :].astype(BF16)
            bias = bias_ref[var]
            zero = jnp.zeros_like(q2)
            accs, ms, ls = [], [], []
            for h in range(2):
                sel = head0 if h == 0 else jnp.logical_not(head0)
                qh = jnp.where(sel, q2, zero).astype(BF16)
                s = _dot_nt(qh, k2) + bias
                m = jnp.max(s, axis=-1, keepdims=True)
                p = jnp.exp2(s - m)
                ls.append(jnp.sum(p, axis=-1, keepdims=True))
                ms.append(m)
                accs.append(_dot(p.astype(BF16), v2))
            acc = jnp.where(head0, accs[0], accs[1])
            m = jnp.where(head0, ms[0], ms[1])
            den = jnp.where(head0, ls[0], ls[1])
            if not first:
                mo = m_ref[rows(qstart, tq), :]
                mn = jnp.maximum(mo, m)
                eo = jnp.exp2(mo - mn)
                eb = jnp.exp2(m - mn)
                den = l_ref[rows(qstart, tq), :] * eo + den * eb
                acc = acc_ref[rows(qstart, tq), :] * eo + acc * eb
                m = mn
            if last:
                o_ref[0, rows(qstart, tq), :] = (acc / den).astype(BF16)
            else:
                m_ref[rows(qstart, tq), :] = m
                l_ref[rows(qstart, tq), :] = den
                acc_ref[rows(qstart, tq), :] = acc
            return carry

        lax.fori_loop(0, d * nblk, body, 0, unroll=8)


def _dilated_attention(qk3, v3):
    b, s, _ = v3.shape
    assert s // max(DILATIONS) >= ATT_W and s % (max(DILATIONS) * ATT_TQ) == 0
    npair = N_HEADS // 2
    blk = lambda part: pl.BlockSpec((1, s, 128), lambda bi, hp: (bi, 0, part * npair + hp))
    seq_f32 = pltpu.VMEM((s, 128), F32)
    return pl.pallas_call(
        functools.partial(_attn_kernel, seq=s),
        grid=(b, npair),
        in_specs=[blk(0), blk(1), blk(0)],
        out_specs=blk(0),
        out_shape=jax.ShapeDtypeStruct((b, s, N_HEADS * HEAD_DIM), BF16),
        scratch_shapes=[pltpu.VMEM((3, ATT_TQ, ATT_W), F32), seq_f32, seq_f32, seq_f32],
        compiler_params=_cparams(("parallel", "parallel")),
        name="attn",
    )(qk3, qk3, v3)


def _mixout_kernel(x_ref, bg_ref, cg_ref, xh_ref, cgp_ref, xhp_ref, cgn_ref, xhn_ref, cw_ref,
                   att_ref, yf_ref, yb_ref, xs_ref, z_ref, ga_ref, gb_ref, gc_ref,
                   d_ref, nw_ref, wa_ref, wb_ref, wc_ref, wo_ref, o_ref):
    i = pl.program_id(1)
    n = pl.num_programs(1)
    cx = cg_ref[0].astype(F32) * xh_ref[0].astype(F32)
    prev = jnp.where(i > 0, (cgp_ref[0].astype(F32) * xhp_ref[0].astype(F32))[HALO - PAD:HALO], 0.0)
    nxt = jnp.where(i < n - 1, (cgn_ref[0].astype(F32) * xhn_ref[0].astype(F32))[0:PAD], 0.0)
    conv = _conv_rows(cx, prev, nxt, cw_ref[...], None)
    ya = _dot((bg_ref[0].astype(F32) * conv).astype(BF16), wa_ref[...])
    yb = _dot(att_ref[0], wb_ref[...])
    y = yf_ref[0].astype(F32) + yb_ref[0].astype(F32) + xs_ref[0].astype(F32) * d_ref[...]
    y = y * z_ref[0].astype(F32)
    gw = SSD_WIDTH // SSD_GROUPS
    parts = []
    for g in range(SSD_GROUPS):
        yg = y[:, g * gw:(g + 1) * gw]
        parts.append(yg * lax.rsqrt(jnp.mean(yg * yg, axis=-1, keepdims=True) + EPS))
    yn = jnp.concatenate(parts, axis=1) * nw_ref[...]
    yc = _dot(yn.astype(BF16), wc_ref[...])
    mix = ga_ref[0].astype(F32) * ya + gb_ref[0].astype(F32) * yb + gc_ref[0].astype(F32) * yc
    o_ref[0] = x_ref[0] + _dot(mix.astype(BF16), wo_ref[...])


def _mixout(x3, a3, g3, zs3, att, yf, yb, xbc_c, conv_a_w, d_exp, norm_w, wa, wb, wc, wo, ts):
    b, s, _ = x3.shape
    hb = ts // HALO
    nhalo = s // HALO

    def cur(width, col):
        return pl.BlockSpec((1, ts, width), lambda bi, i: (bi, i, col))

    def prv(col):
        return pl.BlockSpec((1, HALO, 1024), lambda bi, i: (bi, jnp.maximum(i * hb - 1, 0), col))

    def nxt(col):
        return pl.BlockSpec((1, HALO, 1024), lambda bi, i: (bi, jnp.minimum((i + 1) * hb, nhalo - 1), col))

    def whole(shape):
        return pl.BlockSpec(shape, lambda bi, i: (0, 0), pipeline_mode=pl.Buffered(1))

    in_specs = [
        cur(1024, 0),
        cur(1024, 0), cur(1024, 1), cur(1024, 2),
        prv(1), prv(2), nxt(1), nxt(2),
        whole(conv_a_w.shape),
        cur(1024, 0),
        cur(SSD_WIDTH, 0), cur(SSD_WIDTH, 0),
        cur(SSD_WIDTH, 0),
        cur(SSD_WIDTH, 0),
        cur(1024, 0), cur(1024, 1), cur(1024, 2),
        whole((1, SSD_WIDTH)), whole((1, SSD_WIDTH)),
        whole(wa.shape), whole(wb.shape), whole(wc.shape), whole(wo.shape),
    ]
    args = [x3, a3, a3, a3, a3, a3, a3, a3, conv_a_w,
            att, yf, yb, xbc_c, zs3, g3, g3, g3, d_exp, norm_w, wa, wb, wc, wo]
    return pl.pallas_call(
        _mixout_kernel,
        grid=(b, s // ts),
        in_specs=in_specs,
        out_specs=cur(1024, 0),
        out_shape=jax.ShapeDtypeStruct((b, s, D_MODEL), F32),
        compiler_params=_cparams(("parallel", "parallel")),
        name="mixout",
    )(*args)


def _mlp_kernel(x_ref, nw_ref, w1_ref, w2_ref, fw_ref, o_ref, *, final_norm):
    x = x_ref[...]
    h = _rms(x, nw_ref[...]).astype(BF16)
    a = jnp.maximum(_dot(h, w1_ref[...]), 0.0)
    y = x + _dot((a * a).astype(BF16), w2_ref[...])
    if final_norm:
        y = _rms(y, fw_ref[...])
    o_ref[...] = y


def _mlp(x2, nw, w1, w2, fw, final_norm, tm):
    t = x2.shape[0]
    row = lambda i: (i, 0)

    def whole(shape):
        return pl.BlockSpec(shape, lambda i: (0, 0), pipeline_mode=pl.Buffered(1))

    return pl.pallas_call(
        functools.partial(_mlp_kernel, final_norm=final_norm),
        grid=(t // tm,),
        in_specs=[pl.BlockSpec((tm, D_MODEL), row), whole((1, D_MODEL)), whole(w1.shape), whole(w2.shape),
                  whole((1, D_MODEL))],
        out_specs=pl.BlockSpec((tm, D_MODEL), row),
        out_shape=jax.ShapeDtypeStruct((t, D_MODEL), F32),
        compiler_params=_cparams(("parallel",)),
        name="mlp",
    )(x2, nw, w1, w2, fw)


def _rope_tables(positions):
    half = ROT_DIM // 2
    inv_freq = jnp.power(ROPE_THETA, -jnp.arange(0, ROT_DIM, 2, dtype=F32) / ROT_DIM)
    ang = positions.astype(F32)[..., None] * inv_freq
    cos, sin = jnp.cos(ang), jnp.sin(ang)
    shp = ang.shape[:-1]
    ones = jnp.ones(shp + (HEAD_DIM - ROT_DIM,), F32)
    zer = lambda k: jnp.zeros(shp + (k,), F32)
    c64 = jnp.concatenate([cos, cos, ones], -1)
    sa64 = jnp.concatenate([-sin, zer(HEAD_DIM - half)], -1)
    sb64 = jnp.concatenate([zer(half), sin, zer(HEAD_DIM - ROT_DIM)], -1)
    rep = lambda v: jnp.concatenate([v, v], -1).reshape(-1, 128)
    return rep(c64), rep(sa64), rep(sb64)


def _head_expand(offset):
    r = lax.broadcasted_iota(jnp.int32, (128, SSD_WIDTH), 0)
    c = lax.broadcasted_iota(jnp.int32, (128, SSD_WIDTH), 1)
    return jnp.where(r == c // SSD_HEAD_DIM + offset, 1.0, 0.0).astype(BF16)


def _pad128(v):
    return jnp.pad(v.reshape(1, -1), ((0, 0), (0, 128 - v.size)))


def _layer(x3, tables, p, final_w, tiles):
    b, s, _ = x3.shape
    t = b * s
    x2 = x3.reshape(t, D_MODEL)
    w_in = p["w_in"]
    w_bf = w_in.astype(BF16)
    w_gate = w_in[:, W_GATE:].astype(BF16)
    w_dt = jnp.pad(w_in[:, W_DT:W_GATE], ((0, 0), (0, 128 - 2 * SSD_HEADS)))
    wd_hi = w_dt.astype(BF16)
    wd_lo = (w_dt - wd_hi.astype(F32)).astype(BF16)
    tm, tn = tiles["proj_tm"], tiles["proj_tn"]
    u, dt_raw = _norm(x2, p["mix_norm_w"].reshape(1, -1), wd_hi, wd_lo, tm)
    a_in = _proj(u, w_bf, W_A, 3072, "plain", BF16, tm, tn)
    qk = _proj(u, w_bf, W_QK, 2048, "rope", F32, tm, tn, tables)
    v = _proj(u, w_bf, W_V, 1024, "plain", F32, tm, tn)
    zs = _proj(u, w_bf, W_Z, SSD_WIDTH, "silu", BF16, tm, tn)
    xbc = _proj(u, w_bf, W_XBC, SSD_CONV_DIM, "plain", BF16, tm, tn)
    gates = _proj(u, w_gate, 0, 3072, "sigmoid", BF16, tm, tn)

    xbc_c = _ssdconv(xbc.reshape(b, s, SSD_CONV_DIM), p["ssd_conv_w"], p["ssd_conv_b"].reshape(1, -1),
                     tiles["conv_ts"])
    bias128 = _pad128(p["ssd_dt_bias"])
    a128 = _pad128(-jnp.exp(p["ssd_a_log"]))
    yf, yb = _ssdscan(xbc_c, dt_raw.reshape(b, s, 128), bias128, a128, _head_expand(0), _head_expand(SSD_HEADS),
                      tiles["scan_nb"])

    att = _dilated_attention(qk.reshape(b, s, 2048), v.reshape(b, s, 1024))

    d_exp = jnp.repeat(p["ssd_d"], SSD_HEAD_DIM).reshape(1, -1)
    x_mid = _mixout(x3, a_in.reshape(b, s, 3072), gates.reshape(b, s, 3072), zs.reshape(b, s, SSD_WIDTH),
                    att, yf, yb, xbc_c, p["conv_a_w"], d_exp, p["ssd_norm_w"].reshape(1, -1),
                    p["w_a_out"].astype(BF16), p["w_b_out"].astype(BF16), p["w_c_out"].astype(BF16),
                    p["w_o"].astype(BF16), tiles["mix_ts"])
    fw = final_w if final_w is not None else p["mlp_norm_w"]
    out = _mlp(x_mid.reshape(t, D_MODEL), p["mlp_norm_w"].reshape(1, -1), p["w_ff1"].astype(BF16),
               p["w_ff2"].astype(BF16), fw.reshape(1, -1), final_w is not None, tiles["mlp_tm"])
    return out.reshape(b, s, D_MODEL)


_NAMES = ("mix_norm_w", "w_in", "conv_a_w", "w_a_out", "w_b_out", "ssd_conv_w", "ssd_conv_b", "ssd_a_log",
          "ssd_dt_bias", "ssd_d", "ssd_norm_w", "w_c_out", "w_o", "mlp_norm_w", "w_ff1", "w_ff2")


def _tiles(s):
    return {"proj_tm": min(1024, s), "proj_tn": 1024, "conv_ts": min(512, s), "mix_ts": min(256, s),
            "mlp_tm": min(512, s), "scan_nb": 1}


def kernel(x, positions, mix_norm_w, w_in, conv_a_w, w_a_out, w_b_out, ssd_conv_w, ssd_conv_b, ssd_a_log,
           ssd_dt_bias, ssd_d, ssd_norm_w, w_c_out, w_o, mlp_norm_w, w_ff1, w_ff2, final_norm_w):
    stacked = (mix_norm_w, w_in, conv_a_w, w_a_out, w_b_out, ssd_conv_w, ssd_conv_b, ssd_a_log,
               ssd_dt_bias, ssd_d, ssd_norm_w, w_c_out, w_o, mlp_norm_w, w_ff1, w_ff2)
    depth = w_in.shape[0]
    tables = _rope_tables(positions)
    tiles = _tiles(x.shape[1])
    for l in range(depth):
        p = {name: arr[l] for name, arr in zip(_NAMES, stacked)}
        x = _layer(x, tables, p, final_norm_w if l == depth - 1 else None, tiles)
    return x
```

```python
import functools
import math

import jax
import jax.numpy as jnp
from jax import lax
from jax.experimental import pallas as pl
from jax.experimental.pallas import tpu as pltpu

F32 = jnp.float32
BF16 = jnp.bfloat16

D_MODEL = 1024
EPS = 1e-6
NEG_BIG = -1e30
LOG2E = math.log2(math.e)
EXP2_CAP = 127.0
N_HEADS = 16
HEAD_DIM = 64
ROT_DIM = 16
ROPE_THETA = 500000.0
DILATIONS = (1, 4, 16)
HALF_WINDOW = 64
SSD_WIDTH = 2048
SSD_HEADS = 32
SSD_HEAD_DIM = 64
SSD_GROUPS = 4
SSD_STATE = 128
SSD_CHUNK = 128
SSD_CONV_DIM = SSD_WIDTH + 2 * SSD_GROUPS * SSD_STATE
D_FF = 4 * D_MODEL

W_A, W_QK, W_V, W_Z, W_XBC = 0, 3072, 5120, 6144, 8192
W_DT = W_XBC + SSD_CONV_DIM
W_GATE = W_DT + 2 * SSD_HEADS

VMEM_LIMIT = 56 * 1024 * 1024
HALO = 16
PAD = 8


def _cparams(sem):
    return pltpu.CompilerParams(dimension_semantics=sem, vmem_limit_bytes=VMEM_LIMIT)


def _dot(a, b):
    return jnp.dot(a, b, preferred_element_type=F32)


def _dot_nt(a, b):
    return lax.dot_general(a, b, (((1,), (1,)), ((), ())), preferred_element_type=F32)


def _split2(v):
    hi = v.astype(BF16)
    lo = (v - hi.astype(F32)).astype(BF16)
    return hi, lo


def _split3(v):
    hi = v.astype(BF16)
    r = v - hi.astype(F32)
    mid = r.astype(BF16)
    lo = (r - mid.astype(F32)).astype(BF16)
    return hi, mid, lo


def _rms(x, w):
    return x * lax.rsqrt(jnp.mean(x * x, axis=-1, keepdims=True) + EPS) * w


def _norm_kernel(x_ref, nw_ref, wdh_ref, wdl_ref, u_ref, dt_ref):
    u = _rms(x_ref[...], nw_ref[...])
    uh, ul = _split2(u)
    u_ref[...] = uh
    dt_ref[...] = _dot(uh, wdh_ref[...]) + _dot(uh, wdl_ref[...]) + _dot(ul, wdh_ref[...])


def _norm(x2, nw, wd_hi, wd_lo, tm):
    t = x2.shape[0]
    row = lambda i: (i, 0)
    const = lambda i: (0, 0)
    return pl.pallas_call(
        _norm_kernel,
        grid=(t // tm,),
        in_specs=[pl.BlockSpec((tm, D_MODEL), row), pl.BlockSpec((1, D_MODEL), const),
                  pl.BlockSpec((D_MODEL, 128), const), pl.BlockSpec((D_MODEL, 128), const)],
        out_specs=[pl.BlockSpec((tm, D_MODEL), row), pl.BlockSpec((tm, 128), row)],
        out_shape=[jax.ShapeDtypeStruct((t, D_MODEL), BF16), jax.ShapeDtypeStruct((t, 128), F32)],
        compiler_params=_cparams(("parallel",)),
        name="norm",
    )(x2, nw, wd_hi, wd_lo)


def _proj_kernel(*refs, kind, tn, q_tiles):
    if kind == "rope":
        u_ref, w_ref, cos_ref, sa_ref, sb_ref, o_ref = refs
    else:
        u_ref, w_ref, o_ref = refs
    acc = _dot(u_ref[...], w_ref[...])
    if kind == "sigmoid":
        acc = jax.nn.sigmoid(acc)
    elif kind == "silu":
        acc = acc * jax.nn.sigmoid(acc)
    elif kind == "rope":
        reps = tn // 128
        c = jnp.tile(cos_ref[...], (1, reps))
        sa = jnp.tile(sa_ref[...], (1, reps))
        sb = jnp.tile(sb_ref[...], (1, reps))
        acc = acc * c + pltpu.roll(acc, tn - ROT_DIM // 2, 1) * sa + pltpu.roll(acc, ROT_DIM // 2, 1) * sb
        acc = acc * jnp.where(pl.program_id(1) < q_tiles, HEAD_DIM ** -0.5 * LOG2E, 1.0).astype(F32)
    o_ref[...] = acc.astype(o_ref.dtype)


def _proj(u, w_bf, col0, width, kind, out_dtype, tm, tn, tables=()):
    t = u.shape[0]
    assert col0 % tn == 0 and width % tn == 0
    j0 = col0 // tn
    row = lambda i, j: (i, 0)
    in_specs = [pl.BlockSpec((tm, D_MODEL), row), pl.BlockSpec((D_MODEL, tn), lambda i, j: (0, j0 + j))]
    in_specs += [pl.BlockSpec((tm, 128), row) for _ in tables]
    return pl.pallas_call(
        functools.partial(_proj_kernel, kind=kind, tn=tn, q_tiles=1024 // tn),
        grid=(t // tm, width // tn),
        in_specs=in_specs,
        out_specs=pl.BlockSpec((tm, tn), lambda i, j: (i, j)),
        out_shape=jax.ShapeDtypeStruct((t, width), out_dtype),
        compiler_params=_cparams(("parallel", "parallel")),
        name=f"proj_{kind}",
    )(u, w_bf, *tables)


def _conv_rows(x, prev, nxt, w, acc):
    ts = x.shape[0]
    xe = jnp.concatenate([prev, x, nxt], axis=0)
    n = xe.shape[0]
    k = w.shape[0]
    for tap in range(k):
        off = tap - k // 2
        sh = xe if off == 0 else pltpu.roll(xe, (-off) % n, 0)
        term = sh[PAD:PAD + ts] * w[tap:tap + 1]
        acc = term if acc is None else acc + term
    return acc


CONV_SUB = 128
CONV_HALO = 64


def _ssdconv_kernel(xc_ref, xp_ref, xn_ref, sh_ref, w_ref, b_ref, o_ref, xe_ref):
    i = pl.program_id(1)
    n = pl.num_programs(1)
    ts = xc_ref.shape[1]
    hal, sub = CONV_HALO, CONV_SUB
    xe_ref[0:hal] = jnp.where(i > 0, xp_ref[0], jnp.zeros_like(xp_ref[0]))
    xe_ref[hal:hal + ts] = xc_ref[0]
    xe_ref[hal + ts:hal + ts + hal] = jnp.where(i < n - 1, xn_ref[0], jnp.zeros_like(xn_ref[0]))
    w = w_ref[...]
    k = w.shape[0]
    for r in range(ts // sub):
        win = xe_ref[r * sub:r * sub + sub + 2 * hal]
        shifted = _dot(sh_ref[...], win)
        y = b_ref[...] + xc_ref[0, r * sub:(r + 1) * sub].astype(F32) * w[k // 2:k // 2 + 1]
        slab = 0
        for tap in range(k):
            if tap != k // 2:
                y = y + shifted[slab * sub:(slab + 1) * sub] * w[tap:tap + 1]
                slab += 1
        o_ref[0, r * sub:(r + 1) * sub] = (y * jax.nn.sigmoid(y)).astype(BF16)


def _shift_matrix(k):
    offs = [tap - k // 2 for tap in range(k) if tap != k // 2]
    t = lax.broadcasted_iota(jnp.int32, (len(offs), CONV_SUB, CONV_SUB + 2 * CONV_HALO), 1)
    j = lax.broadcasted_iota(jnp.int32, (len(offs), CONV_SUB, CONV_SUB + 2 * CONV_HALO), 2)
    off = jnp.asarray(offs, jnp.int32).reshape(-1, 1, 1)
    return jnp.where(j == t + off + CONV_HALO, 1.0, 0.0).astype(BF16).reshape(len(offs) * CONV_SUB, -1)


def _ssdconv(xbc3, conv_w, conv_b, ts):
    b, s, _ = xbc3.shape
    nblk = SSD_CONV_DIM // 1024
    hb = ts // CONV_HALO
    nhalo = s // CONV_HALO
    k = conv_w.shape[0]
    shift = _shift_matrix(k)
    return pl.pallas_call(
        _ssdconv_kernel,
        grid=(b, s // ts, nblk),
        in_specs=[
            pl.BlockSpec((1, ts, 1024), lambda bi, i, c: (bi, i, c)),
            pl.BlockSpec((1, CONV_HALO, 1024), lambda bi, i, c: (bi, jnp.maximum(i * hb - 1, 0), c)),
            pl.BlockSpec((1, CONV_HALO, 1024), lambda bi, i, c: (bi, jnp.minimum((i + 1) * hb, nhalo - 1), c)),
            pl.BlockSpec(shift.shape, lambda bi, i, c: (0, 0)),
            pl.BlockSpec((k, 1024), lambda bi, i, c: (0, c)),
            pl.BlockSpec((1, 1024), lambda bi, i, c: (0, c)),
        ],
        out_specs=pl.BlockSpec((1, ts, 1024), lambda bi, i, c: (bi, i, c)),
        out_shape=jax.ShapeDtypeStruct((b, s, SSD_CONV_DIM), BF16),
        scratch_shapes=[pltpu.VMEM((ts + 2 * CONV_HALO, 1024), BF16)],
        compiler_params=_cparams(("parallel", "parallel", "parallel")),
        name="ssdconv",
    )(xbc3, xbc3, xbc3, shift, conv_w, conv_b)


def _softplus(v):
    return jnp.maximum(v, 0.0) + jnp.log(1.0 + jnp.exp(-jnp.abs(v)))


def _ssd_direction(bi, xbc_ref, dtraw, bias, a, expand, lane_off, reverse, state_ref, out_ref):
    t = SSD_CHUNK
    li = lax.broadcasted_iota(jnp.int32, (t, t), 0)
    si = lax.broadcasted_iota(jnp.int32, (t, t), 1)
    keep = (si >= li) if reverse else (si <= li)
    tri = jnp.where(keep, 1.0, 0.0).astype(BF16)

    dt = _softplus(dtraw + bias)
    la = dt * a
    h3 = _split3(la)
    cum = _dot(tri, h3[0]) + _dot(tri, h3[1]) + _dot(tri, h3[2])
    edge = cum[0:1, :] if reverse else cum[t - 1:t, :]
    cum2 = cum * LOG2E
    src_t = (cum2 - jnp.log(dt) * LOG2E).T
    ecx = _dot(jnp.exp(cum).astype(BF16), expand)
    wsx = _dot((dt * jnp.exp(edge - cum)).astype(BF16), expand)
    xw = xbc_ref[bi, :, 0:SSD_WIDTH] * wsx.astype(BF16)
    edge_row = t - 1 if not reverse else 0
    chunk_decay = ecx[edge_row:edge_row + 1, :]

    lane = lax.broadcasted_iota(jnp.int32, (t, 128), 1)
    low = lane < SSD_HEAD_DIM
    gw = SSD_WIDTH // SSD_GROUPS
    for g in range(SSD_GROUPS):
        bg = xbc_ref[bi, :, SSD_WIDTH + g * SSD_STATE:SSD_WIDTH + (g + 1) * SSD_STATE]
        cg = xbc_ref[bi, :, SSD_WIDTH + (SSD_GROUPS + g) * SSD_STATE:SSD_WIDTH + (SSD_GROUPS + g + 1) * SSD_STATE]
        cb = jnp.where(keep, _dot_nt(cg, bg), 0.0)
        st = state_ref[bi, g]
        yoff = _dot(cg, st.astype(BF16)) * ecx[:, g * gw:(g + 1) * gw]
        for pair in range(gw // 128):
            h0 = g * (gw // SSD_HEAD_DIM) + 2 * pair
            c0 = h0 * SSD_HEAD_DIM
            xp = xbc_ref[bi, :, c0:c0 + 128]
            zero = jnp.zeros_like(xp)
            ms = []
            for sub in range(2):
                hl = lane_off + h0 + sub
                seg = jnp.minimum(cum2[:, hl:hl + 1] - src_t[hl:hl + 1, :], EXP2_CAP)
                ms.append((cb * jnp.exp2(seg)).astype(BF16))
            lhs = jnp.concatenate(ms, axis=1)
            rhs = jnp.concatenate([jnp.where(low, xp, zero), jnp.where(low, zero, xp)], axis=0)
            out_ref[bi, :, c0:c0 + 128] = (yoff[:, pair * 128:(pair + 1) * 128] + _dot(lhs, rhs)).astype(BF16)
        bg_t = bg.astype(F32).T.astype(BF16)
        state_ref[bi, g] = st * chunk_decay[:, g * gw:(g + 1) * gw] + _dot(bg_t, xw[:, g * gw:(g + 1) * gw])


def _ssdscan_kernel(xf_ref, xb_ref, dtf_ref, dtb_ref, bias_ref, a_ref, ef_ref, eb_ref,
                    yf_ref, yb_ref, sf_ref, sb_ref):
    @pl.when(pl.program_id(1) == 0)
    def _():
        sf_ref[...] = jnp.zeros_like(sf_ref)
        sb_ref[...] = jnp.zeros_like(sb_ref)

    bias = bias_ref[...]
    a = a_ref[...]
    for bi in range(xf_ref.shape[0]):
        _ssd_direction(bi, xf_ref, dtf_ref[bi], bias, a, ef_ref[...], 0, False, sf_ref, yf_ref)
        _ssd_direction(bi, xb_ref, dtb_ref[bi], bias, a, eb_ref[...], SSD_HEADS, True, sb_ref, yb_ref)


def _ssdscan(xbc_c, dt_raw3, bias128, a128, exp_f, exp_b, nb):
    b, s, _ = xbc_c.shape
    nc = s // SSD_CHUNK
    fwd = lambda bi, c: (bi, c, 0)
    bwd = lambda bi, c: (bi, nc - 1 - c, 0)
    const = lambda bi, c: (0, 0)
    state = pltpu.VMEM((nb, SSD_GROUPS, SSD_STATE, SSD_WIDTH // SSD_GROUPS), F32)
    return pl.pallas_call(
        _ssdscan_kernel,
        grid=(b // nb, nc),
        in_specs=[
            pl.BlockSpec((nb, SSD_CHUNK, SSD_CONV_DIM), fwd),
            pl.BlockSpec((nb, SSD_CHUNK, SSD_CONV_DIM), bwd),
            pl.BlockSpec((nb, SSD_CHUNK, 128), fwd),
            pl.BlockSpec((nb, SSD_CHUNK, 128), bwd),
            pl.BlockSpec((1, 128), const),
            pl.BlockSpec((1, 128), const),
            pl.BlockSpec((128, SSD_WIDTH), const),
            pl.BlockSpec((128, SSD_WIDTH), const),
        ],
        out_specs=[
            pl.BlockSpec((nb, SSD_CHUNK, SSD_WIDTH), fwd),
            pl.BlockSpec((nb, SSD_CHUNK, SSD_WIDTH), bwd),
        ],
        out_shape=[
            jax.ShapeDtypeStruct((b, s, SSD_WIDTH), BF16),
            jax.ShapeDtypeStruct((b, s, SSD_WIDTH), BF16),
        ],
        scratch_shapes=[state, state],
        compiler_params=_cparams(("parallel", "arbitrary")),
        name="ssdscan",
    )(xbc_c, xbc_c, dt_raw3, dt_raw3, bias128, a128, exp_f, exp_b)


ATT_TQ = 128
ATT_W = ATT_TQ + 2 * HALF_WINDOW


def _attn_kernel(q_ref, k_ref, v_ref, o_ref, bias_ref, m_ref, l_ref, acc_ref, *, seq):
    tq, w, hw = ATT_TQ, ATT_W, HALF_WINDOW
    ql = lax.broadcasted_iota(jnp.int32, (tq, w), 0)
    kj = lax.broadcasted_iota(jnp.int32, (tq, w), 1)
    for v in range(3):
        bias_ref[v] = jnp.where(jnp.abs(kj - ql - hw * v) <= hw, 0.0, NEG_BIG).astype(F32)
    lane = lax.broadcasted_iota(jnp.int32, (tq, 128), 1)
    head0 = lane < HEAD_DIM

    order = tuple(reversed(DILATIONS))
    for pi, d in enumerate(order):
        length = seq // d
        nblk = length // tq
        shift = nblk.bit_length() - 1
        first, last = pi == 0, pi == len(order) - 1

        def rows(start, size, d=d):
            return pl.ds(start, size) if d == 1 else pl.ds(start, size, stride=d)

        def body(idx, carry, d=d, length=length, nblk=nblk, shift=shift, first=first, last=last, rows=rows):
            r = lax.shift_right_logical(idx, shift)
            i = lax.bitwise_and(idx, nblk - 1)
            ksub = jnp.clip(i * tq - hw, 0, length - w)
            var = lax.shift_right_logical(i * tq - ksub, hw.bit_length() - 1)
            qstart = r + d * (i * tq)
            kstart = r + d * ksub
            if d == 1:
                qstart = pl.multiple_of(qstart, tq)
                kstart = pl.multiple_of(kstart, hw)
            q2 = q_ref[0, rows(qstart, tq), :]
            k2 = k_ref[0, rows(kstart, w), :].astype(BF16)
            v2 = v_ref[0, rows(kstart, w), :].astype(BF16)
            bias = bias_ref[var]
            zero = jnp.zeros_like(q2)
            accs, ms, ls = [], [], []
            for h in range(2):
                sel = head0 if h == 0 else jnp.logical_not(head0)
                qh = jnp.where(sel, q2, zero).astype(BF16)
                s = _dot_nt(qh, k2) + bias
                m = jnp.max(s, axis=-1, keepdims=True)
                p = jnp.exp2(s - m)
                ls.append(jnp.sum(p, axis=-1, keepdims=True))
                ms.append(m)
                accs.append(_dot(p.astype(BF16), v2))
            acc = jnp.where(head0, accs[0], accs[1])
            m = jnp.where(head0, ms[0], ms[1])
            den = jnp.where(head0, ls[0], ls[1])
            if not first:
                mo = m_ref[rows(qstart, tq), :]
                mn = jnp.maximum(mo, m)
                eo = jnp.exp2(mo - mn)
                eb = jnp.exp2(m - mn)
                den = l_ref[rows(qstart, tq), :] * eo + den * eb
                acc = acc_ref[rows(qstart, tq), :] * eo + acc * eb
                m = mn
            if last:
                o_ref[0, rows(qstart, tq), :] = (acc / den).astype(BF16)
            else:
                m_ref[rows(qstart, tq), :] = m
                l_ref[rows(qstart, tq), :] = den
                acc_ref[rows(qstart, tq), :] = acc
            return carry

        lax.fori_loop(0, d * nblk, body, 0, unroll=16)


def _dilated_attention(qk3, v3):
    b, s, _ = v3.shape
    assert s // max(DILATIONS) >= ATT_W and s % (max(DILATIONS) * ATT_TQ) == 0
    npair = N_HEADS // 2
    blk = lambda part: pl.BlockSpec((1, s, 128), lambda bi, hp: (bi, 0, part * npair + hp))
    seq_f32 = pltpu.VMEM((s, 128), F32)
    return pl.pallas_call(
        functools.partial(_attn_kernel, seq=s),
        grid=(b, npair),
        in_specs=[blk(0), blk(1), blk(0)],
        out_specs=blk(0),
        out_shape=jax.ShapeDtypeStruct((b, s, N_HEADS * HEAD_DIM), BF16),
        scratch_shapes=[pltpu.VMEM((3, ATT_TQ, ATT_W), F32), seq_f32, seq_f32, seq_f32],
        compiler_params=_cparams(("parallel", "parallel")),
        name="attn",
    )(qk3, qk3, v3)


def _mixout_kernel(x_ref, bg_ref, cg_ref, xh_ref, cgp_ref, xhp_ref, cgn_ref, xhn_ref, cw_ref,
                   att_ref, yf_ref, yb_ref, xs_ref, z_ref, ga_ref, gb_ref, gc_ref,
                   d_ref, nw_ref, wa_ref, wb_ref, wc_ref, wo_ref, o_ref):
    i = pl.program_id(1)
    n = pl.num_programs(1)
    cx = cg_ref[0].astype(F32) * xh_ref[0].astype(F32)
    prev = jnp.where(i > 0, (cgp_ref[0].astype(F32) * xhp_ref[0].astype(F32))[HALO - PAD:HALO], 0.0)
    nxt = jnp.where(i < n - 1, (cgn_ref[0].astype(F32) * xhn_ref[0].astype(F32))[0:PAD], 0.0)
    conv = _conv_rows(cx, prev, nxt, cw_ref[...], None)
    ya = _dot((bg_ref[0].astype(F32) * conv).astype(BF16), wa_ref[...])
    yb = _dot(att_ref[0], wb_ref[...])
    y = yf_ref[0].astype(F32) + yb_ref[0].astype(F32) + xs_ref[0].astype(F32) * d_ref[...]
    y = y * z_ref[0].astype(F32)
    gw = SSD_WIDTH // SSD_GROUPS
    parts = []
    for g in range(SSD_GROUPS):
        yg = y[:, g * gw:(g + 1) * gw]
        parts.append(yg * lax.rsqrt(jnp.mean(yg * yg, axis=-1, keepdims=True) + EPS))
    yn = jnp.concatenate(parts, axis=1) * nw_ref[...]
    yc = _dot(yn.astype(BF16), wc_ref[...])
    mix = ga_ref[0].astype(F32) * ya + gb_ref[0].astype(F32) * yb + gc_ref[0].astype(F32) * yc
    o_ref[0] = x_ref[0] + _dot(mix.astype(BF16), wo_ref[...])


def _mixout(x3, a3, g3, zs3, att, yf, yb, xbc_c, conv_a_w, d_exp, norm_w, wa, wb, wc, wo, ts):
    b, s, _ = x3.shape
    hb = ts // HALO
    nhalo = s // HALO

    def cur(width, col):
        return pl.BlockSpec((1, ts, width), lambda bi, i: (bi, i, col))

    def prv(col):
        return pl.BlockSpec((1, HALO, 1024), lambda bi, i: (bi, jnp.maximum(i * hb - 1, 0), col))

    def nxt(col):
        return pl.BlockSpec((1, HALO, 1024), lambda bi, i: (bi, jnp.minimum((i + 1) * hb, nhalo - 1), col))

    def whole(shape):
        return pl.BlockSpec(shape, lambda bi, i: (0, 0), pipeline_mode=pl.Buffered(1))

    in_specs = [
        cur(1024, 0),
        cur(1024, 0), cur(1024, 1), cur(1024, 2),
        prv(1), prv(2), nxt(1), nxt(2),
        whole(conv_a_w.shape),
        cur(1024, 0),
        cur(SSD_WIDTH, 0), cur(SSD_WIDTH, 0),
        cur(SSD_WIDTH, 0),
        cur(SSD_WIDTH, 0),
        cur(1024, 0), cur(1024, 1), cur(1024, 2),
        whole((1, SSD_WIDTH)), whole((1, SSD_WIDTH)),
        whole(wa.shape), whole(wb.shape), whole(wc.shape), whole(wo.shape),
    ]
    args = [x3, a3, a3, a3, a3, a3, a3, a3, conv_a_w,
            att, yf, yb, xbc_c, zs3, g3, g3, g3, d_exp, norm_w, wa, wb, wc, wo]
    return pl.pallas_call(
        _mixout_kernel,
        grid=(b, s // ts),
        in_specs=in_specs,
        out_specs=cur(1024, 0),
        out_shape=jax.ShapeDtypeStruct((b, s, D_MODEL), F32),
        compiler_params=_cparams(("parallel", "parallel")),
        name="mixout",
    )(*args)


def _mlp_kernel(x_ref, nw_ref, w1_ref, w2_ref, fw_ref, o_ref, *, final_norm):
    x = x_ref[...]
    h = _rms(x, nw_ref[...]).astype(BF16)
    a = jnp.maximum(_dot(h, w1_ref[...]), 0.0)
    y = x + _dot((a * a).astype(BF16), w2_ref[...])
    if final_norm:
        y = _rms(y, fw_ref[...])
    o_ref[...] = y


def _mlp(x2, nw, w1, w2, fw, final_norm, tm):
    t = x2.shape[0]
    row = lambda i: (i, 0)

    def whole(shape):
        return pl.BlockSpec(shape, lambda i: (0, 0), pipeline_mode=pl.Buffered(1))

    return pl.pallas_call(
        functools.partial(_mlp_kernel, final_norm=final_norm),
        grid=(t // tm,),
        in_specs=[pl.BlockSpec((tm, D_MODEL), row), whole((1, D_MODEL)), whole(w1.shape), whole(w2.shape),
                  whole((1, D_MODEL))],
        out_specs=pl.BlockSpec((tm, D_MODEL), row),
        out_shape=jax.ShapeDtypeStruct((t, D_MODEL), F32),
        compiler_params=_cparams(("parallel",)),
        name="mlp",
    )(x2, nw, w1, w2, fw)


def _rope_tables(positions):
    half = ROT_DIM // 2
    c = jnp.arange(128, dtype=jnp.int32) % HEAD_DIM
    inv_freq = jnp.power(ROPE_THETA, -jnp.arange(0, ROT_DIM, 2, dtype=F32) / ROT_DIM)[c % half]
    ang = positions.astype(F32).reshape(-1, 1) * inv_freq
    cos, sin = jnp.cos(ang), jnp.sin(ang)
    c_t = jnp.where(c < ROT_DIM, cos, 1.0)
    sa_t = jnp.where(c < half, -sin, 0.0)
    sb_t = jnp.where((c >= half) & (c < ROT_DIM), sin, 0.0)
    return c_t, sa_t, sb_t


def _head_expand(offset):
    r = lax.broadcasted_iota(jnp.int32, (128, SSD_WIDTH), 0)
    c = lax.broadcasted_iota(jnp.int32, (128, SSD_WIDTH), 1)
    return jnp.where(r == c // SSD_HEAD_DIM + offset, 1.0, 0.0).astype(BF16)


def _pad128(v):
    return jnp.pad(v.reshape(1, -1), ((0, 0), (0, 128 - v.size)))


def _layer(x3, tables, p, final_w, tiles):
    b, s, _ = x3.shape
    t = b * s
    x2 = x3.reshape(t, D_MODEL)
    w_in = p["w_in"]
    w_bf = w_in.astype(BF16)
    w_gate = w_in[:, W_GATE:].astype(BF16)
    w_dt = jnp.pad(w_in[:, W_DT:W_GATE], ((0, 0), (0, 128 - 2 * SSD_HEADS)))
    wd_hi = w_dt.astype(BF16)
    wd_lo = (w_dt - wd_hi.astype(F32)).astype(BF16)
    tm, tn = tiles["proj_tm"], tiles["proj_tn"]
    u, dt_raw = _norm(x2, p["mix_norm_w"].reshape(1, -1), wd_hi, wd_lo, tm)
    a_in = _proj(u, w_bf, W_A, 3072, "plain", BF16, tm, tn)
    qk = _proj(u, w_bf, W_QK, 2048, "rope", F32, tm, tn, tables)
    v = _proj(u, w_bf, W_V, 1024, "plain", F32, tm, tn)
    zs = _proj(u, w_bf, W_Z, SSD_WIDTH, "silu", BF16, tm, tn)
    xbc = _proj(u, w_bf, W_XBC, SSD_CONV_DIM, "plain", BF16, tm, tn)
    gates = _proj(u, w_gate, 0, 3072, "sigmoid", BF16, tm, tn)

    xbc_c = _ssdconv(xbc.reshape(b, s, SSD_CONV_DIM), p["ssd_conv_w"], p["ssd_conv_b"].reshape(1, -1),
                     tiles["conv_ts"])
    bias128 = _pad128(p["ssd_dt_bias"])
    a128 = _pad128(-jnp.exp(p["ssd_a_log"]))
    yf, yb = _ssdscan(xbc_c, dt_raw.reshape(b, s, 128), bias128, a128, _head_expand(0), _head_expand(SSD_HEADS),
                      tiles["scan_nb"])

    att = _dilated_attention(qk.reshape(b, s, 2048), v.reshape(b, s, 1024))

    d_exp = jnp.repeat(p["ssd_d"], SSD_HEAD_DIM).reshape(1, -1)
    x_mid = _mixout(x3, a_in.reshape(b, s, 3072), gates.reshape(b, s, 3072), zs.reshape(b, s, SSD_WIDTH),
                    att, yf, yb, xbc_c, p["conv_a_w"], d_exp, p["ssd_norm_w"].reshape(1, -1),
                    p["w_a_out"].astype(BF16), p["w_b_out"].astype(BF16), p["w_c_out"].astype(BF16),
                    p["w_o"].astype(BF16), tiles["mix_ts"])
    fw = final_w if final_w is not None else p["mlp_norm_w"]
    out = _mlp(x_mid.reshape(t, D_MODEL), p["mlp_norm_w"].reshape(1, -1), p["w_ff1"].astype(BF16),
               p["w_ff2"].astype(BF16), fw.reshape(1, -1), final_w is not None, tiles["mlp_tm"])
    return out.reshape(b, s, D_MODEL)


_NAMES = ("mix_norm_w", "w_in", "conv_a_w", "w_a_out", "w_b_out", "ssd_conv_w", "ssd_conv_b", "ssd_a_log",
          "ssd_dt_bias", "ssd_d", "ssd_norm_w", "w_c_out", "w_o", "mlp_norm_w", "w_ff1", "w_ff2")


def _tiles(s):
    return {"proj_tm": min(2048, s), "proj_tn": 1024, "conv_ts": min(512, s), "mix_ts": min(256, s),
            "mlp_tm": min(512, s), "scan_nb": 1}


def kernel(x, positions, mix_norm_w, w_in, conv_a_w, w_a_out, w_b_out, ssd_conv_w, ssd_conv_b, ssd_a_log,
           ssd_dt_bias, ssd_d, ssd_norm_w, w_c_out, w_o, mlp_norm_w, w_ff1, w_ff2, final_norm_w):
    stacked = (mix_norm_w, w_in, conv_a_w, w_a_out, w_b_out, ssd_conv_w, ssd_conv_b, ssd_a_log,
               ssd_dt_bias, ssd_d, ssd_norm_w, w_c_out, w_o, mlp_norm_w, w_ff1, w_ff2)
    depth = w_in.shape[0]
    tables = _rope_tables(positions)
    tiles = _tiles(x.shape[1])
    for l in range(depth):
        p = {name: arr[l] for name, arr in zip(_NAMES, stacked)}
        x = _layer(x, tables, p, final_norm_w if l == depth - 1 else None, tiles)
    return x
```

```python
import functools
import math

import jax
import jax.numpy as jnp
from jax import lax
from jax.experimental import pallas as pl
from jax.experimental.pallas import tpu as pltpu

F32 = jnp.float32
BF16 = jnp.bfloat16

D_MODEL = 1024
EPS = 1e-6
NEG_BIG = -1e30
LOG2E = math.log2(math.e)
EXP2_CAP = 127.0
N_HEADS = 16
HEAD_DIM = 64
ROT_DIM = 16
ROPE_THETA = 500000.0
DILATIONS = (1, 4, 16)
HALF_WINDOW = 64
SSD_WIDTH = 2048
SSD_HEADS = 32
SSD_HEAD_DIM = 64
SSD_GROUPS = 4
SSD_STATE = 128
SSD_CHUNK = 128
SSD_CONV_DIM = SSD_WIDTH + 2 * SSD_GROUPS * SSD_STATE

W_A, W_QK, W_V, W_Z, W_XBC = 0, 3072, 5120, 6144, 8192
W_DT = W_XBC + SSD_CONV_DIM
W_GATE = W_DT + 2 * SSD_HEADS

VMEM_LIMIT = 56 * 1024 * 1024
HALO = 16
PAD = 8


def _cparams(sem, flags=None):
    return pltpu.CompilerParams(dimension_semantics=sem, vmem_limit_bytes=VMEM_LIMIT, flags=flags)


def _dot(a, b):
    return jnp.dot(a, b, preferred_element_type=F32)


def _dot_nt(a, b):
    return lax.dot_general(a, b, (((1,), (1,)), ((), ())), preferred_element_type=F32)


def _split2(v):
    hi = v.astype(BF16)
    lo = (v - hi.astype(F32)).astype(BF16)
    return hi, lo


def _split3(v):
    hi = v.astype(BF16)
    r = v - hi.astype(F32)
    mid = r.astype(BF16)
    lo = (r - mid.astype(F32)).astype(BF16)
    return hi, mid, lo


def _sigmoid(v):
    return 0.5 * jnp.tanh(0.5 * v) + 0.5


def _rms(x, w):
    return x * lax.rsqrt(jnp.mean(x * x, axis=-1, keepdims=True) + EPS) * w


def _norm_kernel(x_ref, nw_ref, wdh_ref, wdl_ref, u_ref, dt_ref):
    u = _rms(x_ref[...], nw_ref[...])
    uh, ul = _split2(u)
    u_ref[...] = uh
    dt_ref[...] = _dot(uh, wdh_ref[...]) + _dot(uh, wdl_ref[...]) + _dot(ul, wdh_ref[...])


def _norm(x2, nw, wd_hi, wd_lo, tm):
    t = x2.shape[0]
    row = lambda i: (i, 0)
    const = lambda i: (0, 0)
    return pl.pallas_call(
        _norm_kernel,
        grid=(t // tm,),
        in_specs=[pl.BlockSpec((tm, D_MODEL), row), pl.BlockSpec((1, D_MODEL), const),
                  pl.BlockSpec((D_MODEL, 128), const), pl.BlockSpec((D_MODEL, 128), const)],
        out_specs=[pl.BlockSpec((tm, D_MODEL), row), pl.BlockSpec((tm, 128), row)],
        out_shape=[jax.ShapeDtypeStruct((t, D_MODEL), BF16), jax.ShapeDtypeStruct((t, 128), F32)],
        compiler_params=_cparams(("parallel",)),
        name="norm",
    )(x2, nw, wd_hi, wd_lo)


def _proj_kernel(*refs, kind, tn, q_cols):
    if kind == "rope":
        u_ref, w_ref, cos_ref, sa_ref, sb_ref, o_ref = refs
    else:
        u_ref, w_ref, o_ref = refs
    acc = _dot(u_ref[...], w_ref[...])
    if kind == "sigmoid":
        acc = _sigmoid(acc)
    elif kind == "silu":
        acc = acc * _sigmoid(acc)
    elif kind == "rope":
        reps = tn // 128
        c = jnp.tile(cos_ref[...], (1, reps))
        sa = jnp.tile(sa_ref[...], (1, reps))
        sb = jnp.tile(sb_ref[...], (1, reps))
        acc = acc * c + pltpu.roll(acc, tn - ROT_DIM // 2, 1) * sa + pltpu.roll(acc, ROT_DIM // 2, 1) * sb
        col = lax.broadcasted_iota(jnp.int32, (1, tn), 1) + pl.program_id(1) * tn
        acc = acc * jnp.where(col < q_cols, HEAD_DIM ** -0.5 * LOG2E, 1.0).astype(F32)
    o_ref[...] = acc.astype(o_ref.dtype)


def _proj(u, w_bf, col0, width, kind, out_dtype, tm, tn, tables=()):
    t = u.shape[0]
    assert col0 % tn == 0 and width % tn == 0
    j0 = col0 // tn
    row = lambda i, j: (i, 0)
    in_specs = [pl.BlockSpec((tm, D_MODEL), row), pl.BlockSpec((D_MODEL, tn), lambda i, j: (0, j0 + j))]
    in_specs += [pl.BlockSpec((tm, 128), row) for _ in tables]
    return pl.pallas_call(
        functools.partial(_proj_kernel, kind=kind, tn=tn, q_cols=N_HEADS * HEAD_DIM),
        grid=(t // tm, width // tn),
        in_specs=in_specs,
        out_specs=pl.BlockSpec((tm, tn), lambda i, j: (i, j)),
        out_shape=jax.ShapeDtypeStruct((t, width), out_dtype),
        compiler_params=_cparams(("parallel", "parallel")),
        name=f"proj_{kind}",
    )(u, w_bf, *tables)


def _conv_rows(x, prev, nxt, w, acc):
    ts = x.shape[0]
    xe = jnp.concatenate([prev, x, nxt], axis=0)
    n = xe.shape[0]
    k = w.shape[0]
    for tap in range(k):
        off = tap - k // 2
        sh = xe if off == 0 else pltpu.roll(xe, (-off) % n, 0)
        term = sh[PAD:PAD + ts] * w[tap:tap + 1]
        acc = term if acc is None else acc + term
    return acc


CONV_SUB = 128
CONV_WIN = CONV_SUB + 2 * HALO


def _projconv_kernel(um_ref, up_ref, un_ref, w_ref, sh_ref, cw_ref, cb_ref, o_ref, xe_ref, *, tiles_per_seq):
    pos = lax.rem(pl.program_id(0), tiles_per_seq)
    tm = um_ref.shape[0]
    lhs = jnp.concatenate([up_ref[...], um_ref[...], un_ref[...]], axis=0)
    xb = _dot(lhs, w_ref[...]).astype(BF16)
    zero = jnp.zeros((HALO, xb.shape[1]), BF16)
    xe_ref[0:HALO] = jnp.where(pos > 0, xb[0:HALO], zero)
    xe_ref[HALO:HALO + tm] = xb[HALO:HALO + tm]
    xe_ref[HALO + tm:] = jnp.where(pos < tiles_per_seq - 1, xb[HALO + tm:], zero)
    w = cw_ref[...]
    k = w.shape[0]

    for r in range(tm // CONV_SUB):
        base = r * CONV_SUB
        shifted = _dot(sh_ref[...], xe_ref[base:base + CONV_WIN, :])
        y = cb_ref[...] + xe_ref[base + HALO:base + HALO + CONV_SUB, :].astype(F32) * w[k // 2:k // 2 + 1]
        slab = 0
        for tap in range(k):
            if tap != k // 2:
                y = y + shifted[slab * CONV_SUB:(slab + 1) * CONV_SUB] * w[tap:tap + 1]
                slab += 1
        o_ref[base:base + CONV_SUB, :] = (y * jax.nn.sigmoid(y)).astype(BF16)


def _shift_matrix(k):
    offs = [tap - k // 2 for tap in range(k) if tap != k // 2]
    t = lax.broadcasted_iota(jnp.int32, (len(offs), CONV_SUB, CONV_WIN), 1)
    j = lax.broadcasted_iota(jnp.int32, (len(offs), CONV_SUB, CONV_WIN), 2)
    off = jnp.asarray(offs, jnp.int32).reshape(-1, 1, 1)
    return jnp.where(j == t + off + HALO, 1.0, 0.0).astype(BF16).reshape(len(offs) * CONV_SUB, -1)


def _projconv(u, w_bf, col0, width, conv_w, conv_b, seq, tm, tn):
    t = u.shape[0]
    assert col0 % tn == 0 and width % tn == 0 and seq % tm == 0 and tm % CONV_SUB == 0
    j0 = col0 // tn
    hb = tm // HALO
    nhalo = t // HALO
    k = conv_w.shape[0]
    assert k // 2 <= HALO
    shift = _shift_matrix(k)
    return pl.pallas_call(
        functools.partial(_projconv_kernel, tiles_per_seq=seq // tm),
        grid=(t // tm, width // tn),
        in_specs=[
            pl.BlockSpec((tm, D_MODEL), lambda i, j: (i, 0)),
            pl.BlockSpec((HALO, D_MODEL), lambda i, j: (jnp.maximum(i * hb - 1, 0), 0)),
            pl.BlockSpec((HALO, D_MODEL), lambda i, j: (jnp.minimum((i + 1) * hb, nhalo - 1), 0)),
            pl.BlockSpec((D_MODEL, tn), lambda i, j: (0, j0 + j)),
            pl.BlockSpec(shift.shape, lambda i, j: (0, 0)),
            pl.BlockSpec((k, tn), lambda i, j: (0, j)),
            pl.BlockSpec((1, tn), lambda i, j: (0, j)),
        ],
        out_specs=pl.BlockSpec((tm, tn), lambda i, j: (i, j)),
        out_shape=jax.ShapeDtypeStruct((t, width), BF16),
        scratch_shapes=[pltpu.VMEM((tm + 2 * HALO, tn), BF16)],
        compiler_params=_cparams(("parallel", "parallel")),
        name="projconv",
    )(u, u, u, w_bf, shift, conv_w, conv_b)


def _softplus(v):
    return jnp.maximum(v, 0.0) + jnp.log(1.0 + jnp.exp(-jnp.abs(v)))


def _ssdprep_kernel(dt_ref, bias_ref, a_ref, cum2_ref, srct_ref, ecum_ref, wst_ref):
    t = SSD_CHUNK
    li = lax.broadcasted_iota(jnp.int32, (t, t), 0)
    si = lax.broadcasted_iota(jnp.int32, (t, t), 1)
    tri_f = jnp.where(si <= li, 1.0, 0.0).astype(BF16)
    tri_b = jnp.where(si >= li, 1.0, 0.0).astype(BF16)
    fwd_lane = lax.broadcasted_iota(jnp.int32, (t, 128), 1) < SSD_HEADS
    for c in range(dt_ref.shape[1] // t):
        rows = slice(c * t, (c + 1) * t)
        dt = _softplus(dt_ref[0, rows] + bias_ref[...])
        h3 = _split3(dt * a_ref[...])
        cum = jnp.where(fwd_lane,
                        _dot(tri_f, h3[0]) + _dot(tri_f, h3[1]) + _dot(tri_f, h3[2]),
                        _dot(tri_b, h3[0]) + _dot(tri_b, h3[1]) + _dot(tri_b, h3[2]))
        edge = jnp.where(fwd_lane[0:1], cum[t - 1:t], cum[0:1])
        cum2 = cum * LOG2E
        cum2_ref[0, rows] = cum2
        srct_ref[0, rows] = (cum2 - jnp.log(dt) * LOG2E).T
        ecum_ref[0, rows] = jnp.exp(cum).astype(BF16)
        wst_ref[0, rows] = (dt * jnp.exp(edge - cum)).astype(BF16)


def _ssdprep(dt_raw3, bias128, a128, tp):
    b, s, _ = dt_raw3.shape
    blk = pl.BlockSpec((1, tp, 128), lambda bi, i: (bi, i, 0))
    const = pl.BlockSpec((1, 128), lambda bi, i: (0, 0))
    return pl.pallas_call(
        _ssdprep_kernel,
        grid=(b, s // tp),
        in_specs=[blk, const, const],
        out_specs=[blk, blk, blk, blk],
        out_shape=[jax.ShapeDtypeStruct((b, s, 128), F32), jax.ShapeDtypeStruct((b, s, 128), F32),
                   jax.ShapeDtypeStruct((b, s, 128), BF16), jax.ShapeDtypeStruct((b, s, 128), BF16)],
        compiler_params=_cparams(("parallel", "parallel")),
        name="ssdprep",
    )(dt_raw3, bias128, a128)


def _ssd_direction(bi, xbc_ref, cum2_ref, srct_ref, ecum_ref, wst_ref, expand, lane_off, reverse, state_ref, out_ref):
    t = SSD_CHUNK
    li = lax.broadcasted_iota(jnp.int32, (t, t), 0)
    si = lax.broadcasted_iota(jnp.int32, (t, t), 1)
    keep = (si >= li) if reverse else (si <= li)
    cum2 = cum2_ref[bi]
    src_t = srct_ref[bi]
    ecx = _dot(ecum_ref[bi], expand)
    wsx = _dot(wst_ref[bi], expand)
    xw = xbc_ref[bi, :, 0:SSD_WIDTH] * wsx.astype(BF16)
    edge_row = t - 1 if not reverse else 0
    chunk_decay = ecx[edge_row:edge_row + 1, :]

    lane = lax.broadcasted_iota(jnp.int32, (t, 128), 1)
    low = lane < SSD_HEAD_DIM
    gw = SSD_WIDTH // SSD_GROUPS
    for g in range(SSD_GROUPS):
        bg = xbc_ref[bi, :, SSD_WIDTH + g * SSD_STATE:SSD_WIDTH + (g + 1) * SSD_STATE]
        cg = xbc_ref[bi, :, SSD_WIDTH + (SSD_GROUPS + g) * SSD_STATE:SSD_WIDTH + (SSD_GROUPS + g + 1) * SSD_STATE]
        cb = jnp.where(keep, _dot_nt(cg, bg), 0.0)
        st = state_ref[bi, g]
        yoff = _dot(cg, st.astype(BF16)) * ecx[:, g * gw:(g + 1) * gw]
        for pair in range(gw // 128):
            h0 = g * (gw // SSD_HEAD_DIM) + 2 * pair
            c0 = h0 * SSD_HEAD_DIM
            xp = xbc_ref[bi, :, c0:c0 + 128]
            zero = jnp.zeros_like(xp)
            ms = []
            for sub in range(2):
                hl = lane_off + h0 + sub
                seg = jnp.minimum(cum2[:, hl:hl + 1] - src_t[hl:hl + 1, :], EXP2_CAP)
                ms.append((cb * jnp.exp2(seg)).astype(BF16))
            lhs = jnp.concatenate(ms, axis=1)
            rhs = jnp.concatenate([jnp.where(low, xp, zero), jnp.where(low, zero, xp)], axis=0)
            out_ref[bi, :, c0:c0 + 128] = (yoff[:, pair * 128:(pair + 1) * 128] + _dot(lhs, rhs)).astype(BF16)
        bg_t = bg.astype(F32).T.astype(BF16)
        state_ref[bi, g] = st * chunk_decay[:, g * gw:(g + 1) * gw] + _dot(bg_t, xw[:, g * gw:(g + 1) * gw])


def _ssdscan_kernel(xf_ref, xb_ref, c2f_ref, c2b_ref, stf_ref, stb_ref, ecf_ref, ecb_ref, wsf_ref, wsb_ref,
                    ef_ref, eb_ref, yf_ref, yb_ref, sf_ref, sb_ref):
    @pl.when(pl.program_id(1) == 0)
    def _():
        sf_ref[...] = jnp.zeros_like(sf_ref)
        sb_ref[...] = jnp.zeros_like(sb_ref)

    for bi in range(xf_ref.shape[0]):
        _ssd_direction(bi, xf_ref, c2f_ref, stf_ref, ecf_ref, wsf_ref, ef_ref[...], 0, False, sf_ref, yf_ref)
        _ssd_direction(bi, xb_ref, c2b_ref, stb_ref, ecb_ref, wsb_ref, eb_ref[...], SSD_HEADS, True, sb_ref, yb_ref)


def _ssdscan(xbc_c, prep, exp_f, exp_b, nb):
    b, s, _ = xbc_c.shape
    nc = s // SSD_CHUNK
    fwd = lambda bi, c: (bi, c, 0)
    bwd = lambda bi, c: (bi, nc - 1 - c, 0)
    const = lambda bi, c: (0, 0)
    state = pltpu.VMEM((nb, SSD_GROUPS, SSD_STATE, SSD_WIDTH // SSD_GROUPS), F32)
    small = []
    for _ in prep:
        small += [pl.BlockSpec((nb, SSD_CHUNK, 128), fwd), pl.BlockSpec((nb, SSD_CHUNK, 128), bwd)]
    return pl.pallas_call(
        _ssdscan_kernel,
        grid=(b // nb, nc),
        in_specs=[
            pl.BlockSpec((nb, SSD_CHUNK, SSD_CONV_DIM), fwd),
            pl.BlockSpec((nb, SSD_CHUNK, SSD_CONV_DIM), bwd),
            *small,
            pl.BlockSpec((128, SSD_WIDTH), const),
            pl.BlockSpec((128, SSD_WIDTH), const),
        ],
        out_specs=[
            pl.BlockSpec((nb, SSD_CHUNK, SSD_WIDTH), fwd),
            pl.BlockSpec((nb, SSD_CHUNK, SSD_WIDTH), bwd),
        ],
        out_shape=[
            jax.ShapeDtypeStruct((b, s, SSD_WIDTH), BF16),
            jax.ShapeDtypeStruct((b, s, SSD_WIDTH), BF16),
        ],
        scratch_shapes=[state, state],
        compiler_params=_cparams(("parallel", "arbitrary")),
        name="ssdscan",
    )(xbc_c, xbc_c, *[arr for arr in prep for _ in range(2)], exp_f, exp_b)


ATT_TQ = 128
ATT_W = ATT_TQ + 2 * HALF_WINDOW
ATT_UNROLL_ROWS = 4096


def _attn_kernel(q_ref, k_ref, v_ref, o_ref, bias_ref, biasw_ref, m_ref, l_ref, acc_ref, *, seq):
    tq, w, hw = ATT_TQ, ATT_W, HALF_WINDOW
    ql = lax.broadcasted_iota(jnp.int32, (tq, w), 0)
    kj = lax.broadcasted_iota(jnp.int32, (tq, w), 1)
    for v in range(3):
        bias_ref[v] = jnp.where(jnp.abs(kj - ql - hw * v) <= hw, 0.0, NEG_BIG).astype(F32)
    qw = lax.broadcasted_iota(jnp.int32, (w, w), 0)
    kw = lax.broadcasted_iota(jnp.int32, (w, w), 1)
    biasw_ref[...] = jnp.where(jnp.abs(kw - qw) <= hw, 0.0, NEG_BIG).astype(F32)

    order = tuple(reversed(DILATIONS))
    for pi, d in enumerate(order):
        length = seq // d
        whole = length == w
        tqd = length if whole else tq
        nblk = length // tqd
        shift = nblk.bit_length() - 1
        first, last = pi == 0, pi == len(order) - 1
        head0 = lax.broadcasted_iota(jnp.int32, (tqd, 128), 1) < HEAD_DIM

        def rows(start, size, d=d):
            return pl.ds(start, size) if d == 1 else pl.ds(start, size, stride=d)

        def body(idx, carry, d=d, length=length, whole=whole, tqd=tqd, nblk=nblk, shift=shift, first=first,
                 last=last, head0=head0, rows=rows):
            r = lax.shift_right_logical(idx, shift)
            i = lax.bitwise_and(idx, nblk - 1)
            if whole:
                qstart = kstart = r
                bias = biasw_ref[...]
            else:
                ksub = jnp.clip(i * tqd - hw, 0, length - w)
                qstart = r + d * (i * tqd)
                kstart = r + d * ksub
                bias = bias_ref[lax.shift_right_logical(i * tqd - ksub, hw.bit_length() - 1)]
            if d == 1:
                qstart = pl.multiple_of(qstart, tqd)
                kstart = pl.multiple_of(kstart, hw)
            q2 = q_ref[0, rows(qstart, tqd), :]
            k2 = k_ref[0, rows(kstart, w), :].astype(BF16)
            v2 = v_ref[0, rows(kstart, w), :].astype(BF16)
            zero = jnp.zeros_like(q2)
            accs, ms, ls = [], [], []
            for h in range(2):
                sel = head0 if h == 0 else jnp.logical_not(head0)
                qh = jnp.where(sel, q2, zero).astype(BF16)
                s = _dot_nt(qh, k2) + bias
                m = jnp.max(s, axis=-1, keepdims=True)
                p = jnp.exp2(s - m)
                ls.append(jnp.sum(p, axis=-1, keepdims=True))
                ms.append(m)
                accs.append(_dot(p.astype(BF16), v2))
            acc = jnp.where(head0, accs[0], accs[1])
            m = jnp.where(head0, ms[0], ms[1])
            den = jnp.where(head0, ls[0], ls[1])
            if not first:
                mo = m_ref[rows(qstart, tqd), :]
                mn = jnp.maximum(mo, m)
                eo = jnp.exp2(mo - mn)
                eb = jnp.exp2(m - mn)
                den = l_ref[rows(qstart, tqd), :] * eo + den * eb
                acc = acc_ref[rows(qstart, tqd), :] * eo + acc * eb
                m = mn
            if last:
                o_ref[0, rows(qstart, tqd), :] = (acc / den).astype(BF16)
            else:
                m_ref[rows(qstart, tqd), :] = m
                l_ref[rows(qstart, tqd), :] = den
                acc_ref[rows(qstart, tqd), :] = acc
            return carry

        lax.fori_loop(0, d * nblk, body, 0, unroll=ATT_UNROLL_ROWS // tqd)


def _dilated_attention(qk3, v3):
    b, s, _ = v3.shape
    assert s // max(DILATIONS) >= ATT_W and s % (max(DILATIONS) * ATT_TQ) == 0
    npair = N_HEADS // 2
    blk = lambda part: pl.BlockSpec((1, s, 128), lambda bi, hp: (bi, 0, part * npair + hp))
    seq_f32 = pltpu.VMEM((s, 128), F32)
    return pl.pallas_call(
        functools.partial(_attn_kernel, seq=s),
        grid=(b, npair),
        in_specs=[blk(0), blk(1), blk(0)],
        out_specs=blk(0),
        out_shape=jax.ShapeDtypeStruct((b, s, N_HEADS * HEAD_DIM), BF16),
        scratch_shapes=[pltpu.VMEM((3, ATT_TQ, ATT_W), F32), pltpu.VMEM((ATT_W, ATT_W), F32),
                        seq_f32, seq_f32, seq_f32],
        compiler_params=_cparams(("parallel", "parallel")),
        name="attn",
    )(qk3, qk3, v3)


def _mixout_kernel(x_ref, bg_ref, cg_ref, xh_ref, cgp_ref, xhp_ref, cgn_ref, xhn_ref, cw_ref,
                   att_ref, yf_ref, yb_ref, xs_ref, z_ref, ga_ref, gb_ref, gc_ref,
                   d_ref, nw_ref, wa_ref, wb_ref, wc_ref, wo_ref, o_ref):
    i = pl.program_id(1)
    n = pl.num_programs(1)
    kc = D_MODEL // 4
    ya = None
    for k in range(4):
        cols = slice(k * kc, (k + 1) * kc)
        cx = cg_ref[0, :, cols].astype(F32) * xh_ref[0, :, cols].astype(F32)
        prev = jnp.where(i > 0, (cgp_ref[0, :, cols].astype(F32) * xhp_ref[0, :, cols].astype(F32))[HALO - PAD:HALO],
                         0.0)
        nxt = jnp.where(i < n - 1, (cgn_ref[0, :, cols].astype(F32) * xhn_ref[0, :, cols].astype(F32))[0:PAD], 0.0)
        conv = _conv_rows(cx, prev, nxt, cw_ref[:, cols], None)
        part = _dot((bg_ref[0, :, cols].astype(F32) * conv).astype(BF16), wa_ref[cols, :])
        ya = part if ya is None else ya + part
    yb = _dot(att_ref[0], wb_ref[...])
    y = yf_ref[0].astype(F32) + yb_ref[0].astype(F32) + xs_ref[0].astype(F32) * d_ref[...]
    y = y * z_ref[0].astype(F32)
    gw = SSD_WIDTH // SSD_GROUPS
    yc = None
    for g in range(SSD_GROUPS):
        yg = y[:, g * gw:(g + 1) * gw]
        yn = yg * lax.rsqrt(jnp.mean(yg * yg, axis=-1, keepdims=True) + EPS) * nw_ref[:, g * gw:(g + 1) * gw]
        part = _dot(yn.astype(BF16), wc_ref[g * gw:(g + 1) * gw, :])
        yc = part if yc is None else yc + part
    out = x_ref[0]
    for k in range(4):
        cols = slice(k * kc, (k + 1) * kc)
        mix = (ga_ref[0, :, cols].astype(F32) * ya[:, cols] + gb_ref[0, :, cols].astype(F32) * yb[:, cols]
               + gc_ref[0, :, cols].astype(F32) * yc[:, cols])
        out = out + _dot(mix.astype(BF16), wo_ref[cols, :])
    o_ref[0] = out


def _mixout(x3, a3, g3, zs3, att, yf, yb, xbc_c, conv_a_w, d_exp, norm_w, wa, wb, wc, wo, ts):
    b, s, _ = x3.shape
    hb = ts // HALO
    nhalo = s // HALO

    def cur(width, col):
        return pl.BlockSpec((1, ts, width), lambda bi, i: (bi, i, col))

    def prv(col):
        return pl.BlockSpec((1, HALO, 1024), lambda bi, i: (bi, jnp.maximum(i * hb - 1, 0), col))

    def nxt(col):
        return pl.BlockSpec((1, HALO, 1024), lambda bi, i: (bi, jnp.minimum((i + 1) * hb, nhalo - 1), col))

    def whole(shape):
        return pl.BlockSpec(shape, lambda bi, i: (0, 0), pipeline_mode=pl.Buffered(1))

    in_specs = [
        cur(1024, 0),
        cur(1024, 0), cur(1024, 1), cur(1024, 2),
        prv(1), prv(2), nxt(1), nxt(2),
        whole(conv_a_w.shape),
        cur(1024, 0),
        cur(SSD_WIDTH, 0), cur(SSD_WIDTH, 0),
        cur(SSD_WIDTH, 0),
        cur(SSD_WIDTH, 0),
        cur(1024, 0), cur(1024, 1), cur(1024, 2),
        whole((1, SSD_WIDTH)), whole((1, SSD_WIDTH)),
        whole(wa.shape), whole(wb.shape), whole(wc.shape), whole(wo.shape),
    ]
    args = [x3, a3, a3, a3, a3, a3, a3, a3, conv_a_w,
            att, yf, yb, xbc_c, zs3, g3, g3, g3, d_exp, norm_w, wa, wb, wc, wo]
    return pl.pallas_call(
        _mixout_kernel,
        grid=(b, s // ts),
        in_specs=in_specs,
        out_specs=cur(1024, 0),
        out_shape=jax.ShapeDtypeStruct((b, s, D_MODEL), F32),
        compiler_params=_cparams(("parallel", "parallel")),
        name="mixout",
    )(*args)


def _mlp_kernel(x_ref, nw_ref, w1_ref, w2_ref, fw_ref, o_ref, *, final_norm):
    x = x_ref[...]
    h = _rms(x, nw_ref[...]).astype(BF16)
    a = jnp.maximum(_dot(h, w1_ref[...]), 0.0)
    y = x + _dot((a * a).astype(BF16), w2_ref[...])
    if final_norm:
        y = _rms(y, fw_ref[...])
    o_ref[...] = y


def _mlp(x2, nw, w1, w2, fw, final_norm, tm):
    t = x2.shape[0]
    row = lambda i: (i, 0)

    def whole(shape):
        return pl.BlockSpec(shape, lambda i: (0, 0), pipeline_mode=pl.Buffered(1))

    return pl.pallas_call(
        functools.partial(_mlp_kernel, final_norm=final_norm),
        grid=(t // tm,),
        in_specs=[pl.BlockSpec((tm, D_MODEL), row), whole((1, D_MODEL)), whole(w1.shape), whole(w2.shape),
                  whole((1, D_MODEL))],
        out_specs=pl.BlockSpec((tm, D_MODEL), row),
        out_shape=jax.ShapeDtypeStruct((t, D_MODEL), F32),
        compiler_params=_cparams(("parallel",)),
        name="mlp",
    )(x2, nw, w1, w2, fw)


def _rope_tables(positions):
    half = ROT_DIM // 2
    c = jnp.arange(128, dtype=jnp.int32) % HEAD_DIM
    inv_freq = jnp.power(ROPE_THETA, -jnp.arange(0, ROT_DIM, 2, dtype=F32) / ROT_DIM)
    ang = inv_freq.reshape(-1, 1) * positions.astype(F32).reshape(1, -1)
    cos = jnp.tile(jnp.cos(ang).T, (1, 128 // half))
    sin = jnp.tile(jnp.sin(ang).T, (1, 128 // half))
    c_t = jnp.where(c < ROT_DIM, cos, 1.0)
    sa_t = jnp.where(c < half, -sin, 0.0)
    sb_t = jnp.where((c >= half) & (c < ROT_DIM), sin, 0.0)
    return c_t, sa_t, sb_t


def _head_expand(offset):
    r = lax.broadcasted_iota(jnp.int32, (128, SSD_WIDTH), 0)
    c = lax.broadcasted_iota(jnp.int32, (128, SSD_WIDTH), 1)
    return jnp.where(r == c // SSD_HEAD_DIM + offset, 1.0, 0.0).astype(BF16)


def _pad128(v):
    return jnp.pad(v.reshape(1, -1), ((0, 0), (0, 128 - v.size)))


def _layer(x3, tables, p, final_w, tiles):
    b, s, _ = x3.shape
    t = b * s
    x2 = x3.reshape(t, D_MODEL)
    w_in = p["w_in"]
    w_bf = w_in.astype(BF16)
    w_gate = w_in[:, W_GATE:].astype(BF16)
    w_dt = jnp.pad(w_in[:, W_DT:W_GATE], ((0, 0), (0, 128 - 2 * SSD_HEADS)))
    wd_hi = w_dt.astype(BF16)
    wd_lo = (w_dt - wd_hi.astype(F32)).astype(BF16)
    tm, tn = tiles["proj_tm"], tiles["proj_tn"]
    u, dt_raw = _norm(x2, p["mix_norm_w"].reshape(1, -1), wd_hi, wd_lo, tm)
    a_in = _proj(u, w_bf, W_A, 3072, "plain", BF16, tm, tn)
    tme = tiles["proj_epi_tm"]
    qk = _proj(u, w_bf[:, W_QK:W_V], 0, 2048, "rope", F32, tme, 2048, tables)
    v = _proj(u, w_bf, W_V, 1024, "plain", F32, tm, tn)
    zs = _proj(u, w_bf, W_Z, SSD_WIDTH, "silu", BF16, tme, SSD_WIDTH)
    gates = _proj(u, w_gate, 0, 3072, "sigmoid", BF16, tme, 3072)
    xbc_c = _projconv(u, w_bf, W_XBC, SSD_CONV_DIM, p["ssd_conv_w"], p["ssd_conv_b"].reshape(1, -1), s, tm, tn)
    xbc_c = xbc_c.reshape(b, s, SSD_CONV_DIM)
    bias128 = _pad128(p["ssd_dt_bias"])
    a128 = _pad128(-jnp.exp(p["ssd_a_log"]))
    prep = _ssdprep(dt_raw.reshape(b, s, 128), bias128, a128, tiles["prep_ts"])
    yf, yb = _ssdscan(xbc_c, prep, _head_expand(0), _head_expand(SSD_HEADS), math.gcd(b, tiles["scan_nb"]))

    att = _dilated_attention(qk.reshape(b, s, 2048), v.reshape(b, s, 1024))

    d_exp = jnp.repeat(p["ssd_d"], SSD_HEAD_DIM).reshape(1, -1)
    x_mid = _mixout(x3, a_in.reshape(b, s, 3072), gates.reshape(b, s, 3072), zs.reshape(b, s, SSD_WIDTH),
                    att, yf, yb, xbc_c, p["conv_a_w"], d_exp, p["ssd_norm_w"].reshape(1, -1),
                    p["w_a_out"].astype(BF16), p["w_b_out"].astype(BF16), p["w_c_out"].astype(BF16),
                    p["w_o"].astype(BF16), tiles["mix_ts"])
    fw = final_w if final_w is not None else p["mlp_norm_w"]
    out = _mlp(x_mid.reshape(t, D_MODEL), p["mlp_norm_w"].reshape(1, -1), p["w_ff1"].astype(BF16),
               p["w_ff2"].astype(BF16), fw.reshape(1, -1), final_w is not None, tiles["mlp_tm"])
    return out.reshape(b, s, D_MODEL)


_NAMES = ("mix_norm_w", "w_in", "conv_a_w", "w_a_out", "w_b_out", "ssd_conv_w", "ssd_conv_b", "ssd_a_log",
          "ssd_dt_bias", "ssd_d", "ssd_norm_w", "w_c_out", "w_o", "mlp_norm_w", "w_ff1", "w_ff2")


def _tiles(s):
    return {"proj_tm": min(2048, s), "proj_epi_tm": min(1024, s), "proj_tn": 1024, "prep_ts": min(512, s), "mix_ts": min(256, s),
            "mlp_tm": min(512, s), "scan_nb": 4}


def kernel(x, positions, mix_norm_w, w_in, conv_a_w, w_a_out, w_b_out, ssd_conv_w, ssd_conv_b, ssd_a_log,
           ssd_dt_bias, ssd_d, ssd_norm_w, w_c_out, w_o, mlp_norm_w, w_ff1, w_ff2, final_norm_w):
    stacked = (mix_norm_w, w_in, conv_a_w, w_a_out, w_b_out, ssd_conv_w, ssd_conv_b, ssd_a_log,
               ssd_dt_bias, ssd_d, ssd_norm_w, w_c_out, w_o, mlp_norm_w, w_ff1, w_ff2)
    depth = w_in.shape[0]
    tables = _rope_tables(positions)
    tiles = _tiles(x.shape[1])
    for l in range(depth):
        p = {name: arr[l] for name, arr in zip(_NAMES, stacked)}
        x = _layer(x, tables, p, final_norm_w if l == depth - 1 else None, tiles)
    return x
```
